```python
import math
import jax, jax.numpy as jnp
from jax import lax
import numpy as np

D_MODEL = 1024
BATCH = 1
SEQ = 16384
DEPTH = 4

GRID_W = 64
CTX_LEN = 256
N_MIXERS = 3
BLOCK = 128
ROPE_BASE = 10000.0
RMS_EPS = 1e-6
N_MOD = 6

A_HEADS = 8
A_KV_HEADS = 2
A_GROUP = A_HEADS // A_KV_HEADS
A_HEAD_DIM = D_MODEL // A_HEADS
A_WINDOW = 128
A_SCALE = A_HEAD_DIM ** -0.5
A_QKV = (A_HEADS + 2 * A_KV_HEADS) * A_HEAD_DIM

B_WIDTH = D_MODEL
B_BLOCKS = 4
B_BLOCK_W = B_WIDTH // B_BLOCKS
B_CONV_W = 4
B_CONV_PAD_L = (B_CONV_W - 1) // 2
B_LRU_C = 8.0

C_HEADS = 8
C_NOPE = 128
C_ROPE = 64
C_VDIM = 128
C_Q_RANK = D_MODEL // 2
C_KV_RANK = D_MODEL // 4
C_SCALE = (C_NOPE + C_ROPE) ** -0.5
C_IN = C_Q_RANK + C_KV_RANK + C_ROPE

FFN_HIDDEN = ((8 * D_MODEL + 3 * 256 - 1) // (3 * 256)) * 256

kernel_name = 'hybrid_dit_swa_rglru_mla_interleaved'


def rms_norm(x, g):
    xf = x.astype(jnp.float32)
    y = xf * lax.rsqrt(jnp.mean(xf * xf, axis=-1, keepdims=True) + RMS_EPS)
    return (y * g.astype(jnp.float32)).astype(x.dtype)


def modulate(h, shift, scale):
    return h * (1 + scale) + shift


def axial_rope_tables(n_tokens, d_rot):
    n_rows = n_tokens // GRID_W
    row = jnp.repeat(jnp.arange(n_rows), GRID_W).astype(jnp.float32)
    col = jnp.tile(jnp.arange(GRID_W), n_rows).astype(jnp.float32)
    d_axis = d_rot // 2
    inv = 1.0 / (ROPE_BASE ** (jnp.arange(0, d_axis, 2, dtype=jnp.float32) / d_axis))
    ang = jnp.stack([row[:, None] * inv, col[:, None] * inv], axis=1)
    return jnp.cos(ang), jnp.sin(ang)


def apply_axial_rope(x, cos, sin):
    shp = x.shape
    xf = x.astype(jnp.float32).reshape(shp[:-1] + (2, 2, shp[-1] // 4))
    x1 = xf[..., 0, :]
    x2 = xf[..., 1, :]
    c = cos[None, :, None]
    s = sin[None, :, None]
    out = jnp.stack([x1 * c - x2 * s, x2 * c + x1 * s], axis=-2)
    return out.reshape(shp).astype(x.dtype)


def swiglu(h, w_gu, w_down):
    z = h @ w_gu
    return (jax.nn.silu(z[..., :FFN_HIDDEN]) * z[..., FFN_HIDDEN:]) @ w_down


def windowed_gqa(h, hc, w_qkv, sinks, w_o, cos, sin, need_ctx):
    B, T, _ = h.shape
    nb = T // BLOCK
    q_w = A_HEADS * A_HEAD_DIM
    kv_w = A_KV_HEADS * A_HEAD_DIM

    def project(u):
        L = u.shape[1]
        z = u @ w_qkv
        q = z[..., :q_w].reshape(B, L, A_HEADS, A_HEAD_DIM)
        k = z[..., q_w:q_w + kv_w].reshape(B, L, A_KV_HEADS, A_HEAD_DIM)
        v = z[..., q_w + kv_w:].reshape(B, L, A_KV_HEADS, A_HEAD_DIM)
        return q, k, v

    q, k, v = project(h)
    q = apply_axial_rope(q, cos, sin)
    k = apply_axial_rope(k, cos, sin)
    qc, kc, vc = project(hc)
    n_ctx = kc.shape[1]
    sink = sinks.astype(jnp.float32).reshape(A_KV_HEADS, A_GROUP)

    qb = q.reshape(B, nb, BLOCK, A_KV_HEADS, A_GROUP, A_HEAD_DIM)

    def band(t):
        tp = jnp.pad(t, ((0, 0), (BLOCK, BLOCK), (0, 0), (0, 0))).reshape(B, nb + 2, BLOCK, A_KV_HEADS, A_HEAD_DIM)
        return jnp.concatenate([tp[:, :-2], tp[:, 1:-1], tp[:, 2:]], axis=2)

    kw = band(k)
    vw = band(v)
    blk = jnp.arange(nb)[:, None, None]
    qpos = blk * BLOCK + jnp.arange(BLOCK)[None, :, None]
    kpos = (blk - 1) * BLOCK + jnp.arange(3 * BLOCK)[None, None, :]
    valid = (jnp.abs(kpos - qpos) <= A_WINDOW) & (kpos >= 0) & (kpos < T)
    s_win = jnp.einsum('bnqkgd,bnjkd->bnkgqj', qb, kw).astype(jnp.float32) * A_SCALE
    s_win = jnp.where(valid[None, :, None, None], s_win, -jnp.inf)
    s_ctx = jnp.einsum('bnqkgd,bckd->bnkgqc', qb, kc).astype(jnp.float32) * A_SCALE
    s_sink = jnp.broadcast_to(sink[None, None, :, :, None, None], s_ctx.shape[:-1] + (1,))
    p = jax.nn.softmax(jnp.concatenate([s_sink, s_ctx, s_win], axis=-1), axis=-1).astype(v.dtype)
    o = (jnp.einsum('bnkgqc,bckd->bnqkgd', p[..., 1:1 + n_ctx], vc)
         + jnp.einsum('bnkgqj,bnjkd->bnqkgd', p[..., 1 + n_ctx:], vw))
    y = o.reshape(B, T, q_w) @ w_o

    if need_ctx:
        qcg = qc.reshape(B, n_ctx, A_KV_HEADS, A_GROUP, A_HEAD_DIM)
        sc = jnp.einsum('bqkgd,bckd->bkgqc', qcg, kc).astype(jnp.float32) * A_SCALE
        ss = jnp.broadcast_to(sink[None, :, :, None, None], sc.shape[:-1] + (1,))
        pc = jax.nn.softmax(jnp.concatenate([ss, sc], axis=-1), axis=-1)[..., 1:].astype(vc.dtype)
        yc = jnp.einsum('bkgqc,bckd->bqkgd', pc, vc).reshape(B, n_ctx, q_w) @ w_o
    else:
        yc = None
    return y, yc


def centred_depthwise_conv(u, w, b):
    T = u.shape[1]
    up = jnp.pad(u, ((0, 0), (B_CONV_PAD_L, B_CONV_W - 1 - B_CONV_PAD_L), (0, 0)))
    return b + sum(w[k] * up[:, k:k + T] for k in range(B_CONV_W))


def block_diag(u, w):
    ub = u.reshape(u.shape[0], u.shape[1], B_BLOCKS, B_BLOCK_W)
    return jnp.einsum('btnk,nkj->btnj', ub, w).reshape(u.shape)


def rglru_coeffs(u, lam, w_a, b_a, w_x, b_x):
    uf = u.astype(jnp.float32)
    r = jax.nn.sigmoid(block_diag(uf, w_a.astype(jnp.float32)) + b_a.astype(jnp.float32))
    i = jax.nn.sigmoid(block_diag(uf, w_x.astype(jnp.float32)) + b_x.astype(jnp.float32))
    log_a = -B_LRU_C * r * jax.nn.softplus(-lam.astype(jnp.float32))
    a = jnp.exp(log_a)
    b = jnp.sqrt(-jnp.expm1(2.0 * log_a)) * (i * uf)
    return a, b


def linear_scan(a, b, h0, reverse):
    if reverse:
        a = jnp.flip(a, 1)
        b = jnp.flip(b, 1)

    def combine(l, r):
        return (l[0] * r[0], r[0] * l[1] + r[1])

    acc_a, acc_b = lax.associative_scan(combine, (a, b), axis=1)
    h = acc_a * h0[:, None, :] + acc_b
    return jnp.flip(h, 1) if reverse else h


def rglru_block(h, hc, w_in, conv_w, conv_b, lam, w_a, b_a, w_x, b_x, w_out, need_ctx):
    B = h.shape[0]

    def branches(u):
        z = u @ w_in
        return centred_depthwise_conv(z[..., :B_WIDTH], conv_w, conv_b), jax.nn.gelu(z[..., B_WIDTH:])

    xb, gate = branches(h)
    xbc, gatec = branches(hc)
    h_sum = 0.0
    hc_sum = 0.0
    for d, reverse in enumerate((False, True)):
        a_c, b_c = rglru_coeffs(xbc, lam[d], w_a[d], b_a[d], w_x[d], b_x[d])
        hs_c = linear_scan(a_c, b_c, jnp.zeros((B, B_WIDTH), jnp.float32), reverse)
        h0 = hs_c[:, 0] if reverse else hs_c[:, -1]
        a_l, b_l = rglru_coeffs(xb, lam[d], w_a[d], b_a[d], w_x[d], b_x[d])
        h_sum = h_sum + linear_scan(a_l, b_l, h0, reverse)
        if need_ctx:
            hc_sum = hc_sum + hs_c
    y = (h_sum.astype(h.dtype) * gate) @ w_out
    yc = (hc_sum.astype(h.dtype) * gatec) @ w_out if need_ctx else None
    return y, yc


def mla(h, hc, w_in, g_q, g_kv, w_uq, w_ukv, w_out, cos, sin, need_ctx):
    B, T, _ = h.shape
    w_ukv3 = w_ukv.reshape(C_KV_RANK, C_HEADS, C_NOPE + C_VDIM)
    w_uk = w_ukv3[..., :C_NOPE]
    w_uv = w_ukv3[..., C_NOPE:]

    def project(u, rotate):
        L = u.shape[1]
        z = u @ w_in
        cq = rms_norm(z[..., :C_Q_RANK], g_q)
        ckv = rms_norm(z[..., C_Q_RANK:C_Q_RANK + C_KV_RANK], g_kv)
        kr = z[..., C_Q_RANK + C_KV_RANK:].reshape(B, L, 1, C_ROPE)
        q = (cq @ w_uq).reshape(B, L, C_HEADS, C_NOPE + C_ROPE)
        qn, qr = q[..., :C_NOPE], q[..., C_NOPE:]
        if rotate:
            qr = apply_axial_rope(qr, cos, sin)
            kr = apply_axial_rope(kr, cos, sin)
        kn = jnp.einsum('blc,chn->blhn', ckv, w_uk)
        v = jnp.einsum('blc,chv->blhv', ckv, w_uv)
        return qn, qr, kn, kr[:, :, 0], v

    qn, qr, kn, kr, v = project(h, True)
    qnc, qrc, knc, krc, vc = project(hc, False)
    kn_all = jnp.concatenate([knc, kn], axis=1)
    kr_all = jnp.concatenate([krc, kr], axis=1)
    v_all = jnp.concatenate([vc, v], axis=1)

    def attend(qn_b, qr_b, kn_, kr_, v_):
        s = (jnp.einsum('bqhn,bkhn->bhqk', qn_b, kn_)
             + jnp.einsum('bqhr,bkr->bhqk', qr_b, kr_)).astype(jnp.float32) * C_SCALE
        p = jax.nn.softmax(s, axis=-1).astype(v_.dtype)
        return jnp.einsum('bhqk,bkhv->bqhv', p, v_)

    nb = T // BLOCK

    def to_blocks(t):
        return jnp.moveaxis(t.reshape((B, nb, BLOCK) + t.shape[2:]), 1, 0)

    o = lax.map(lambda qs: attend(qs[0], qs[1], kn_all, kr_all, v_all), (to_blocks(qn), to_blocks(qr)))
    o = jnp.moveaxis(o, 0, 1).reshape(B, T, C_HEADS * C_VDIM)
    y = o @ w_out
    if need_ctx:
        yc = attend(qnc, qrc, knc, krc, vc).reshape(B, qnc.shape[1], C_HEADS * C_VDIM) @ w_out
    else:
        yc = None
    return y, yc


def setup_inputs(seed: int = 0) -> dict:
    key = jax.random.key(seed)
    ks = iter(jax.random.split(key, 40))
    f32 = jnp.float32
    n_a = (DEPTH + N_MIXERS - 1) // N_MIXERS
    n_b = (DEPTH + N_MIXERS - 2) // N_MIXERS
    n_c = DEPTH // N_MIXERS

    def nrm(shape, fan_in, gain=1.0):
        return gain * fan_in ** -0.5 * jax.random.normal(next(ks), shape, f32)

    def small(shape, s):
        return s * jax.random.normal(next(ks), shape, f32)

    def gains(shape):
        return 1.0 + 0.05 * jax.random.normal(next(ks), shape, f32)

    x = jax.random.normal(next(ks), (BATCH, SEQ, D_MODEL), f32)
    c = jax.random.normal(next(ks), (BATCH, D_MODEL), f32)
    ctx = jax.random.normal(next(ks), (BATCH, CTX_LEN, D_MODEL), f32)
    c_ctx = jax.random.normal(next(ks), (D_MODEL,), f32)

    ada_w = nrm((DEPTH, D_MODEL, N_MOD * D_MODEL), D_MODEL, 0.5)
    ada_b = small((DEPTH, N_MOD * D_MODEL), 0.02)
    norm_g = gains((DEPTH, 4, D_MODEL))
    ffn_w_gu = nrm((DEPTH, D_MODEL, 2 * FFN_HIDDEN), D_MODEL)
    ffn_w_down = nrm((DEPTH, FFN_HIDDEN, D_MODEL), FFN_HIDDEN)

    a_w_qkv = nrm((n_a, D_MODEL, A_QKV), D_MODEL)
    a_sinks = small((n_a, A_HEADS), 1.0)
    a_w_o = nrm((n_a, A_HEADS * A_HEAD_DIM, D_MODEL), A_HEADS * A_HEAD_DIM)

    b_w_in = nrm((n_b, D_MODEL, 2 * B_WIDTH), D_MODEL)
    b_conv_w = nrm((n_b, B_CONV_W, B_WIDTH), B_CONV_W)
    b_conv_b = small((n_b, B_WIDTH), 0.02)
    a_pow = jax.random.uniform(next(ks), (n_b, 2, B_WIDTH), f32, minval=0.9, maxval=0.999)
    p_lam = a_pow ** (1.0 / B_LRU_C)
    b_lam = jnp.log(p_lam) - jnp.log1p(-p_lam)
    b_w_a = nrm((n_b, 2, B_BLOCKS, B_BLOCK_W, B_BLOCK_W), B_BLOCK_W)
    b_b_a = small((n_b, 2, B_WIDTH), 0.1)
    b_w_x = nrm((n_b, 2, B_BLOCKS, B_BLOCK_W, B_BLOCK_W), B_BLOCK_W)
    b_b_x = small((n_b, 2, B_WIDTH), 0.1)
    b_w_out = nrm((n_b, B_WIDTH, D_MODEL), B_WIDTH)

    c_w_in = nrm((n_c, D_MODEL, C_IN), D_MODEL)
    c_g_q = gains((n_c, C_Q_RANK))
    c_g_kv = gains((n_c, C_KV_RANK))
    c_w_uq = nrm((n_c, C_Q_RANK, C_HEADS * (C_NOPE + C_ROPE)), C_Q_RANK)
    c_w_ukv = nrm((n_c, C_KV_RANK, C_HEADS * (C_NOPE + C_VDIM)), C_KV_RANK)
    c_w_out = nrm((n_c, C_HEADS * C_VDIM, D_MODEL), C_HEADS * C_VDIM)

    return {'x': x, 'c': c, 'ctx': ctx, 'c_ctx': c_ctx,
            'ada_w': ada_w, 'ada_b': ada_b, 'norm_g': norm_g,
            'ffn_w_gu': ffn_w_gu, 'ffn_w_down': ffn_w_down,
            'a_w_qkv': a_w_qkv, 'a_sinks': a_sinks, 'a_w_o': a_w_o,
            'b_w_in': b_w_in, 'b_conv_w': b_conv_w, 'b_conv_b': b_conv_b, 'b_lam': b_lam,
            'b_w_a': b_w_a, 'b_b_a': b_b_a, 'b_w_x': b_w_x, 'b_b_x': b_b_x, 'b_w_out': b_w_out,
            'c_w_in': c_w_in, 'c_g_q': c_g_q, 'c_g_kv': c_g_kv, 'c_w_uq': c_w_uq,
            'c_w_ukv': c_w_ukv, 'c_w_out': c_w_out}


def reference(x, c, ctx, c_ctx, ada_w, ada_b, norm_g, ffn_w_gu, ffn_w_down,
              a_w_qkv, a_sinks, a_w_o,
              b_w_in, b_conv_w, b_conv_b, b_lam, b_w_a, b_b_a, b_w_x, b_b_x, b_w_out,
              c_w_in, c_g_q, c_g_kv, c_w_uq, c_w_ukv, c_w_out):
    n_tok = x.shape[1]
    cos_a, sin_a = axial_rope_tables(n_tok, A_HEAD_DIM)
    cos_c, sin_c = axial_rope_tables(n_tok, C_ROPE)
    s_lat = jax.nn.silu(c)
    s_ctx = jax.nn.silu(c_ctx)[None]
    xc = ctx
    for i in range(DEPTH):
        last = i == DEPTH - 1
        kind, j = i % N_MIXERS, i // N_MIXERS
        m = jnp.split((s_lat @ ada_w[i] + ada_b[i])[:, None, :], N_MOD, axis=-1)
        mc = jnp.split((s_ctx @ ada_w[i] + ada_b[i])[:, None, :], N_MOD, axis=-1)
        g = norm_g[i]
        h = modulate(rms_norm(x, g[0]), m[0], m[1])
        hc = modulate(rms_norm(xc, g[0]), mc[0], mc[1])
        if kind == 0:
            y, yc = windowed_gqa(h, hc, a_w_qkv[j], a_sinks[j], a_w_o[j], cos_a, sin_a, not last)
        elif kind == 1:
            y, yc = rglru_block(h, hc, b_w_in[j], b_conv_w[j], b_conv_b[j], b_lam[j], b_w_a[j], b_b_a[j],
                                b_w_x[j], b_b_x[j], b_w_out[j], not last)
        else:
            y, yc = mla(h, hc, c_w_in[j], c_g_q[j], c_g_kv[j], c_w_uq[j], c_w_ukv[j], c_w_out[j],
                        cos_c, sin_c, not last)
        x = x + m[2] * rms_norm(y, g[1])
        x = x + m[5] * rms_norm(swiglu(modulate(rms_norm(x, g[2]), m[3], m[4]), ffn_w_gu[i], ffn_w_down[i]), g[3])
        if not last:
            xc = xc + mc[2] * rms_norm(yc, g[1])
            xc = xc + mc[5] * rms_norm(swiglu(modulate(rms_norm(xc, g[2]), mc[3], mc[4]), ffn_w_gu[i], ffn_w_down[i]), g[3])
    return x
```

```python
import functools
import math

import jax
import jax.numpy as jnp
from jax import lax
from jax.experimental import pallas as pl
from jax.experimental.pallas import tpu as pltpu

D_MODEL = 1024
DEPTH = 4
GRID_W = 64
N_MIXERS = 3
BLOCK = 128
ROPE_BASE = 10000.0
RMS_EPS = 1e-6
N_MOD = 6

A_HEADS = 8
A_KV_HEADS = 2
A_GROUP = A_HEADS // A_KV_HEADS
A_HEAD_DIM = D_MODEL // A_HEADS
A_WINDOW = 128
A_SCALE = A_HEAD_DIM ** -0.5
A_Q_W = A_HEADS * A_HEAD_DIM
A_KV_W = A_KV_HEADS * A_HEAD_DIM
A_QKV = A_Q_W + 2 * A_KV_W

B_WIDTH = D_MODEL
B_BLOCKS = 4
B_BLOCK_W = B_WIDTH // B_BLOCKS
B_CONV_W = 4
B_LRU_C = 8.0

C_HEADS = 8
C_NOPE = 128
C_ROPE = 64
C_VDIM = 128
C_Q_RANK = D_MODEL // 2
C_KV_RANK = D_MODEL // 4
C_SCALE = (C_NOPE + C_ROPE) ** -0.5

FFN_HIDDEN = ((8 * D_MODEL + 3 * 256 - 1) // (3 * 256)) * 256

LANE = 128
SUBLANE = 8
ROW_TILE = 256
FFN_CHUNK = 256
VMEM_LIMIT = 56 * 1024 * 1024
NEG = -1e30

BF16 = jnp.bfloat16
F32 = jnp.float32


def _dot(a, b):
    return jnp.dot(a, b, preferred_element_type=F32)


def _dot_nt(a, b):
    return lax.dot_general(a, b, (((1,), (1,)), ((), ())), preferred_element_type=F32)


def _resident(shape):
    nd = len(shape)
    return pl.BlockSpec(shape, lambda *_: (0,) * nd, pipeline_mode=pl.Buffered(1))


def _params(*sem):
    return pltpu.CompilerParams(dimension_semantics=sem, vmem_limit_bytes=VMEM_LIMIT)


def _rms(x, g):
    ms = jnp.mean(x * x, axis=-1, keepdims=True)
    return x * lax.rsqrt(ms + RMS_EPS) * g


def _norm_mod(x, g, shift, scale):
    return _rms(x, g) * (1.0 + scale) + shift


def _rope(z, cos, sin, half):
    lane = lax.broadcasted_iota(jnp.int32, z.shape, 1)
    first = (lane % (2 * half)) < half
    partner = jnp.where(first, pltpu.roll(z, LANE - half, 1), pltpu.roll(z, half, 1))
    return z * cos + partner * sin


def _ada_kernel(s_ref, w_ref, b_ref, o_ref):
    s = s_ref[...]
    s = s * jax.nn.sigmoid(s)
    o_ref[0] = _dot(s.astype(BF16), w_ref[0].astype(BF16)) + b_ref[0]


def _ada(s_rows, ada_w, ada_b):
    tn = 1536
    depth, d, n = ada_w.shape
    return pl.pallas_call(
        _ada_kernel,
        grid=(depth, n // tn),
        in_specs=[
            pl.BlockSpec((SUBLANE, d), lambda l, j: (0, 0)),
            pl.BlockSpec((1, d, tn), lambda l, j: (l, 0, j)),
            pl.BlockSpec((1, 1, tn), lambda l, j: (l, 0, j)),
        ],
        out_specs=pl.BlockSpec((1, SUBLANE, tn), lambda l, j: (l, 0, j)),
        out_shape=jax.ShapeDtypeStruct((depth, SUBLANE, n), F32),
        compiler_params=_params("parallel", "parallel"),
        name="ada",
    )(s_rows, ada_w, ada_b.reshape(depth, 1, n))


def _mods_spec(n_lat_tiles):
    return pl.BlockSpec((1, N_MOD, D_MODEL), lambda i: (jnp.where(i >= n_lat_tiles, 1, 0), 0, 0))


def _post_kernel(o_ref, x_ref, mods_ref, g_ref, wo_ref, wgu_ref, wd_ref, out_ref, act_ref):
    m = mods_ref[0]
    g = g_ref[...]
    y = _dot(o_ref[...], wo_ref[...])
    x1 = x_ref[...] + m[2:3] * _rms(y, g[1:2])
    h = _norm_mod(x1, g[2:3], m[3:4], m[4:5]).astype(BF16)
    for c in range(FFN_HIDDEN // FFN_CHUNK):
        lo = c * FFN_CHUNK
        zg = _dot(h, wgu_ref[:, lo:lo + FFN_CHUNK])
        zu = _dot(h, wgu_ref[:, FFN_HIDDEN + lo:FFN_HIDDEN + lo + FFN_CHUNK])
        act_ref[:, lo:lo + FFN_CHUNK] = (zg * jax.nn.sigmoid(zg) * zu).astype(BF16)
    f = _dot(act_ref[...], wd_ref[...])
    out_ref[...] = x1 + m[5:6] * _rms(f, g[3:4])


def _post(o, xa, mods, g, w_o, w_gu, w_down, n_lat_tiles, n_tiles):
    d = D_MODEL
    row = lambda i: (i, 0)
    return pl.pallas_call(
        _post_kernel,
        grid=(n_tiles,),
        in_specs=[
            pl.BlockSpec((ROW_TILE, w_o.shape[0]), row),
            pl.BlockSpec((ROW_TILE, d), row),
            _mods_spec(n_lat_tiles),
            _resident(g.shape),
            _resident(w_o.shape),
            _resident(w_gu.shape),
            _resident(w_down.shape),
        ],
        out_specs=pl.BlockSpec((ROW_TILE, d), row),
        out_shape=jax.ShapeDtypeStruct((n_tiles * ROW_TILE, d), F32),
        scratch_shapes=[pltpu.VMEM((ROW_TILE, FFN_HIDDEN), BF16)],
        compiler_params=_params("parallel"),
        name="post_ffn",
    )(o, xa, mods, g, w_o, w_gu, w_down)


def _a_proj_kernel(x_ref, mods_ref, g_ref, w_ref, cos_ref, sin_ref, q_ref, k_ref, v_ref):
    m = mods_ref[0]
    h = _norm_mod(x_ref[...], g_ref[0:1], m[0:1], m[1:2]).astype(BF16)
    z = _dot(h, w_ref[...])
    cos = cos_ref[...]
    sin = sin_ref[...]
    half = A_HEAD_DIM // 4
    for hd in range(A_HEADS):
        lo = hd * A_HEAD_DIM
        q_ref[:, lo:lo + A_HEAD_DIM] = _rope(z[:, lo:lo + A_HEAD_DIM], cos, sin, half).astype(BF16)
    for hd in range(A_KV_HEADS):
        lo = hd * A_HEAD_DIM
        k_ref[:, lo:lo + A_HEAD_DIM] = _rope(z[:, A_Q_W + lo:A_Q_W + lo + A_HEAD_DIM], cos, sin, half).astype(BF16)
    v_ref[...] = z[:, A_Q_W + A_KV_W:].astype(BF16)


def _a_proj(xa, mods, g, w_qkv, cos, sin, n_lat_tiles):
    n_tiles = xa.shape[0] // ROW_TILE
    row = lambda i: (i, 0)
    return pl.pallas_call(
        _a_proj_kernel,
        grid=(n_tiles,),
        in_specs=[
            pl.BlockSpec((ROW_TILE, D_MODEL), row),
            _mods_spec(n_lat_tiles),
            _resident(g.shape),
            _resident(w_qkv.shape),
            pl.BlockSpec((ROW_TILE, LANE), row),
            pl.BlockSpec((ROW_TILE, LANE), row),
        ],
        out_specs=[
            pl.BlockSpec((ROW_TILE, A_Q_W), row),
            pl.BlockSpec((ROW_TILE, A_KV_W), row),
            pl.BlockSpec((ROW_TILE, A_KV_W), row),
        ],
        out_shape=[
            jax.ShapeDtypeStruct((xa.shape[0], A_Q_W), BF16),
            jax.ShapeDtypeStruct((xa.shape[0], A_KV_W), BF16),
            jax.ShapeDtypeStruct((xa.shape[0], A_KV_W), BF16),
        ],
        compiler_params=_params("parallel"),
        name="a_proj",
    )(xa, mods, g, w_qkv, cos, sin)


def _a_attn_kernel(nb, sink_ref, q_ref, kp_ref, kc_ref, kn_ref, kx_ref, vp_ref, vc_ref, vn_ref, vx_ref, o_ref):
    i = pl.program_id(0)
    is_lat = i < nb
    prev_ok = jnp.logical_and(is_lat, i >= 1)
    next_ok = i < nb - 1
    rows = A_GROUP * BLOCK
    r = lax.broadcasted_iota(jnp.int32, (rows, BLOCK), 0) % BLOCK
    c = lax.broadcasted_iota(jnp.int32, (rows, BLOCK), 1)
    mask_p = jnp.logical_and(c >= r, prev_ok)
    mask_n = jnp.logical_and(c <= r, next_ok)
    grp = lax.broadcasted_iota(jnp.int32, (rows, 1), 0) // BLOCK
    for kh in range(A_KV_HEADS):
        ks = slice(kh * A_HEAD_DIM, (kh + 1) * A_HEAD_DIM)
        qg = jnp.concatenate(
            [q_ref[:, (kh * A_GROUP + gi) * A_HEAD_DIM:(kh * A_GROUP + gi + 1) * A_HEAD_DIM] for gi in range(A_GROUP)],
            axis=0)
        sink = jnp.zeros((rows, 1), F32)
        for gi in range(A_GROUP):
            sink = jnp.where(grp == gi, sink_ref[kh * A_GROUP + gi], sink)
        s_x = _dot_nt(qg, kx_ref[:, ks]) * A_SCALE
        s_p = jnp.where(mask_p, _dot_nt(qg, kp_ref[:, ks]) * A_SCALE, NEG)
        s_c = jnp.where(is_lat, _dot_nt(qg, kc_ref[:, ks]) * A_SCALE, NEG)
        s_n = jnp.where(mask_n, _dot_nt(qg, kn_ref[:, ks]) * A_SCALE, NEG)
        mx = jnp.maximum(jnp.max(s_x, axis=-1, keepdims=True), jnp.max(s_c, axis=-1, keepdims=True))
        mx = jnp.maximum(mx, jnp.maximum(jnp.max(s_p, axis=-1, keepdims=True), jnp.max(s_n, axis=-1, keepdims=True)))
        mx = jnp.maximum(mx, sink)
        e_x = jnp.exp(s_x - mx)
        e_p = jnp.exp(s_p - mx)
        e_c = jnp.exp(s_c - mx)
        e_n = jnp.exp(s_n - mx)
        den = (jnp.exp(sink - mx) + jnp.sum(e_x, axis=-1, keepdims=True) + jnp.sum(e_p, axis=-1, keepdims=True)
               + jnp.sum(e_c, axis=-1, keepdims=True) + jnp.sum(e_n, axis=-1, keepdims=True))
        o = (_dot(e_x.astype(BF16), vx_ref[:, ks]) + _dot(e_p.astype(BF16), vp_ref[:, ks])
             + _dot(e_c.astype(BF16), vc_ref[:, ks]) + _dot(e_n.astype(BF16), vn_ref[:, ks]))
        o = o / den
        for gi in range(A_GROUP):
            hd = kh * A_GROUP + gi
            o_ref[:, hd * A_HEAD_DIM:(hd + 1) * A_HEAD_DIM] = o[gi * BLOCK:(gi + 1) * BLOCK].astype(BF16)


def _a_attn(q, k, v, sinks, n_lat, n_ctx, with_ctx_queries):
    nb = n_lat // BLOCK
    n_blocks = nb + (n_ctx // BLOCK if with_ctx_queries else 0)
    last = (n_lat + n_ctx) // BLOCK - 1
    cur = lambda i: (i, 0)
    prev = lambda i: (jnp.maximum(i - 1, 0), 0)
    nxt = lambda i: (jnp.minimum(i + 1, last), 0)
    ctx = lambda i: (n_lat // n_ctx, 0)
    kv_specs = [
        pl.BlockSpec((BLOCK, A_KV_W), prev),
        pl.BlockSpec((BLOCK, A_KV_W), cur),
        pl.BlockSpec((BLOCK, A_KV_W), nxt),
        pl.BlockSpec((n_ctx, A_KV_W), ctx),
    ]
    return pl.pallas_call(
        functools.partial(_a_attn_kernel, nb),
        grid=(n_blocks,),
        in_specs=[pl.BlockSpec(memory_space=pltpu.SMEM), pl.BlockSpec((BLOCK, A_Q_W), cur)] + kv_specs + kv_specs,
        out_specs=pl.BlockSpec((BLOCK, A_Q_W), cur),
        out_shape=jax.ShapeDtypeStruct((n_blocks * BLOCK, A_Q_W), BF16),
        compiler_params=_params("parallel"),
        name="a_attn",
    )(sinks, q, k, k, k, k, v, v, v, v)


def _b_proj_kernel(x_ref, mods_ref, g_ref, w_ref, xpre_ref, gate_ref):
    m = mods_ref[0]
    h = _norm_mod(x_ref[...], g_ref[0:1], m[0:1], m[1:2]).astype(BF16)
    xpre_ref[...] = _dot(h, w_ref[:, :B_WIDTH])
    gate_ref[...] = jax.nn.gelu(_dot(h, w_ref[:, B_WIDTH:]), approximate=True)


def _b_proj(xa, mods, g, w_in, n_lat_tiles):
    n_tiles = xa.shape[0] // ROW_TILE
    row = lambda i: (i, 0)
    return pl.pallas_call(
        _b_proj_kernel,
        grid=(n_tiles,),
        in_specs=[
            pl.BlockSpec((ROW_TILE, D_MODEL), row),
            _mods_spec(n_lat_tiles),
            _resident(g.shape),
            _resident(w_in.shape),
        ],
        out_specs=[pl.BlockSpec((ROW_TILE, B_WIDTH), row), pl.BlockSpec((ROW_TILE, B_WIDTH), row)],
        out_shape=[jax.ShapeDtypeStruct((xa.shape[0], B_WIDTH), F32)] * 2,
        compiler_params=_params("parallel"),
        name="b_proj",
    )(xa, mods, g, w_in)


def _softplus(y):
    return jnp.maximum(y, 0.0) + jnp.log1p(jnp.exp(-jnp.abs(y)))


def _expm1(x, u):
    one = u == 1.0
    k = (u - 1.0) * x / jnp.where(one, 1.0, jnp.log(u))
    return jnp.where(one, x, jnp.where(x < -1.0, u - 1.0, k))


def _b_tile_index(reverse, n_lat_tiles, i):
    lat = (n_lat_tiles - i) if reverse else (i - 1)
    return jnp.where(i == 0, n_lat_tiles, lat)


def _b_scan_kernel(reverse, n_lat_tiles, *refs):
    if reverse:
        (xc_ref, xp_ref, xn_ref, cw_ref, cb_ref, lam_ref, wa_ref, ba_ref, wx_ref, bx_ref,
         hf_ref, gate_ref, out_ref, carry_ref) = refs
    else:
        (xc_ref, xp_ref, xn_ref, cw_ref, cb_ref, lam_ref, wa_ref, ba_ref, wx_ref, bx_ref,
         out_ref, carry_ref) = refs
    i = pl.program_id(0)
    tile = _b_tile_index(reverse, n_lat_tiles, i)

    @pl.when(i == 0)
    def _():
        carry_ref[...] = jnp.zeros(carry_ref.shape, F32)

    prev_ok = jnp.logical_and(tile >= 1, tile <= n_lat_tiles - 1)
    next_ok = tile <= n_lat_tiles - 2
    u = xc_ref[...]
    n = u.shape[0]
    row = lax.broadcasted_iota(jnp.int32, (n, 1), 0)
    pm = jnp.where(prev_ok, xp_ref[SUBLANE - 1:SUBLANE, :], 0.0)
    n0 = jnp.where(next_ok, xn_ref[0:1, :], 0.0)
    n1 = jnp.where(next_ok, xn_ref[1:2, :], 0.0)
    um1 = jnp.where(row == 0, pm, pltpu.roll(u, 1, 0))
    up1 = jnp.where(row == n - 1, n0, pltpu.roll(u, n - 1, 0))
    up2 = jnp.where(row == n - 2, n0, jnp.where(row == n - 1, n1, pltpu.roll(u, n - 2, 0)))
    cw = cw_ref[...]
    xb = cb_ref[...] + (cw[0:1] * um1 + cw[1:2] * u + cw[2:3] * up1 + cw[3:4] * up2)

    xb16 = xb.astype(BF16)
    rs, gs = [], []
    for nblk in range(B_BLOCKS):
        sl = slice(nblk * B_BLOCK_W, (nblk + 1) * B_BLOCK_W)
        rs.append(_dot(xb16[:, sl], wa_ref[nblk]))
        gs.append(_dot(xb16[:, sl], wx_ref[nblk]))
    r = jax.nn.sigmoid(jnp.concatenate(rs, axis=1) + ba_ref[...])
    gi = jax.nn.sigmoid(jnp.concatenate(gs, axis=1) + bx_ref[...])
    log_a = (-B_LRU_C * r) * _softplus(-lam_ref[...])
    a = jnp.exp(log_a)
    b = jnp.sqrt(-_expm1(2.0 * log_a, a * a)) * (gi * xb)

    s = 1
    while s < n:
        if reverse:
            shift, valid = n - s, row < n - s
        else:
            shift, valid = s, row >= s
        a_sh = pltpu.roll(a, shift, 0)
        b_sh = pltpu.roll(b, shift, 0)
        b = jnp.where(valid, a * b_sh + b, b)
        a = jnp.where(valid, a * a_sh, a)
        s *= 2
    h = a * carry_ref[0:1, :] + b
    carry_ref[0:1, :] = h[0:1, :] if reverse else h[n - 1:n, :]
    if reverse:
        out_ref[...] = ((hf_ref[...] + h) * gate_ref[...]).astype(out_ref.dtype)
    else:
        out_ref[...] = h


def _b_scan(reverse, xpre, conv_w, conv_b, lam, w_a, b_a, w_x, b_x, n_lat_tiles, hf=None, gate=None):
    n_tiles = xpre.shape[0] // ROW_TILE
    per8 = ROW_TILE // SUBLANE
    last8 = xpre.shape[0] // SUBLANE - 1
    tile = functools.partial(_b_tile_index, reverse, n_lat_tiles)
    cur = lambda i: (tile(i), 0)
    prev8 = lambda i: (jnp.maximum(tile(i) * per8 - 1, 0), 0)
    next8 = lambda i: (jnp.minimum((tile(i) + 1) * per8, last8), 0)
    w = B_WIDTH
    in_specs = [
        pl.BlockSpec((ROW_TILE, w), cur),
        pl.BlockSpec((SUBLANE, w), prev8),
        pl.BlockSpec((SUBLANE, w), next8),
        _resident(conv_w.shape), _resident(conv_b.shape), _resident(lam.shape),
        _resident(w_a.shape), _resident(b_a.shape), _resident(w_x.shape), _resident(b_x.shape),
    ]
    args = [xpre, xpre, xpre, conv_w, conv_b, lam, w_a, b_a, w_x, b_x]
    if reverse:
        in_specs += [pl.BlockSpec((ROW_TILE, w), cur), pl.BlockSpec((ROW_TILE, w), cur)]
        args += [hf, gate]
    return pl.pallas_call(
        functools.partial(_b_scan_kernel, reverse, n_lat_tiles),
        grid=(n_tiles,),
        in_specs=in_specs,
        out_specs=pl.BlockSpec((ROW_TILE, w), cur),
        out_shape=jax.ShapeDtypeStruct(xpre.shape, BF16 if reverse else F32),
        scratch_shapes=[pltpu.VMEM((SUBLANE, w), F32)],
        compiler_params=_params("arbitrary"),
        name="b_scan_rev" if reverse else "b_scan_fwd",
    )(*args)


C_IN_PAD = C_Q_RANK + C_KV_RANK + LANE
C_QR_PAD = C_HEADS * LANE


def _c_proj_kernel(x_ref, mods_ref, g_ref, win_ref, gq_ref, gkv_ref, wuq_ref, wuk_ref, wuvt_ref, cos_ref, sin_ref,
                   qn_ref, qr_ref, kn_ref, kr_ref, vt_ref):
    m = mods_ref[0]
    h = _norm_mod(x_ref[...], g_ref[0:1], m[0:1], m[1:2]).astype(BF16)
    z = _dot(h, win_ref[...])
    cq = _rms(z[:, :C_Q_RANK], gq_ref[...]).astype(BF16)
    ckv = _rms(z[:, C_Q_RANK:C_Q_RANK + C_KV_RANK], gkv_ref[...]).astype(BF16)
    cos = cos_ref[...]
    sin = sin_ref[...]
    half = C_ROPE // 4
    kr_ref[...] = _rope(z[:, C_Q_RANK + C_KV_RANK:], cos, sin, half).astype(BF16)
    q = _dot(cq, wuq_ref[...])
    qn_ref[...] = q[:, :C_HEADS * C_NOPE].astype(BF16)
    for hd in range(C_HEADS):
        lo = C_HEADS * C_NOPE + hd * LANE
        qr_ref[:, hd * LANE:(hd + 1) * LANE] = _rope(q[:, lo:lo + LANE], cos, sin, half).astype(BF16)
    kn_ref[...] = _dot(ckv, wuk_ref[...]).astype(BF16)
    vt_ref[...] = _dot_nt(wuvt_ref[...], ckv).astype(BF16)


def _c_proj(xa, mods, g, w_in, g_q, g_kv, w_uq, w_uk, w_uvt, cos, sin, n_lat_tiles):
    n_rows = xa.shape[0]
    n_tiles = n_rows // ROW_TILE
    row = lambda i: (i, 0)
    hw = C_HEADS * C_NOPE
    return pl.pallas_call(
        _c_proj_kernel,
        grid=(n_tiles,),
        in_specs=[
            pl.BlockSpec((ROW_TILE, D_MODEL), row),
            _mods_spec(n_lat_tiles),
            _resident(g.shape), _resident(w_in.shape), _resident(g_q.shape), _resident(g_kv.shape),
            _resident(w_uq.shape), _resident(w_uk.shape), _resident(w_uvt.shape),
            pl.BlockSpec((ROW_TILE, LANE), row),
            pl.BlockSpec((ROW_TILE, LANE), row),
        ],
        out_specs=[
            pl.BlockSpec((ROW_TILE, hw), row),
            pl.BlockSpec((ROW_TILE, C_QR_PAD), row),
            pl.BlockSpec((ROW_TILE, hw), row),
            pl.BlockSpec((ROW_TILE, LANE), row),
            pl.BlockSpec((C_HEADS * C_VDIM, ROW_TILE), lambda i: (0, i)),
        ],
        out_shape=[
            jax.ShapeDtypeStruct((n_rows, hw), BF16),
            jax.ShapeDtypeStruct((n_rows, C_QR_PAD), BF16),
            jax.ShapeDtypeStruct((n_rows, hw), BF16),
            jax.ShapeDtypeStruct((n_rows, LANE), BF16),
            jax.ShapeDtypeStruct((C_HEADS * C_VDIM, n_rows), BF16),
        ],
        compiler_params=_params("parallel"),
        name="c_proj",
    )(xa, mods, g, w_in, g_q, g_kv, w_uq, w_uk, w_uvt, cos, sin)


def _c_head_step(hd, qn_ref, qr_ref, kn_ref, kr_ref, vt_ref, m_ref, l_ref, acc_ref):
    hs = slice(hd * LANE, (hd + 1) * LANE)
    qcat = jnp.concatenate([qn_ref[:, hs], qr_ref[:, hs]], axis=1)
    kcat = jnp.concatenate([kn_ref[:, hs], kr_ref[...]], axis=1)
    st = _dot_nt(kcat, qcat) * C_SCALE
    m_old = m_ref[hd:hd + 1, :]
    m_new = jnp.maximum(m_old, jnp.max(st, axis=0, keepdims=True))
    alpha = jnp.exp(m_old - m_new)
    p = jnp.exp(st - m_new)
    l_ref[hd:hd + 1, :] = alpha * l_ref[hd:hd + 1, :] + jnp.sum(p, axis=0, keepdims=True)
    acc_ref[hd] = alpha * acc_ref[hd] + _dot(vt_ref[hs, :], p.astype(BF16))
    m_ref[hd:hd + 1, :] = m_new


def _c_init(m_ref, l_ref, acc_ref):
    m_ref[...] = jnp.full(m_ref.shape, NEG, F32)
    l_ref[...] = jnp.zeros(l_ref.shape, F32)
    acc_ref[...] = jnp.zeros(acc_ref.shape, F32)


def _c_finish(l_ref, acc_ref, o_ref):
    for hd in range(C_HEADS):
        ot = acc_ref[hd] / l_ref[hd:hd + 1, :]
        o_ref[:, hd * C_VDIM:(hd + 1) * C_VDIM] = ot.T.astype(o_ref.dtype)


def _c_attn_lat_kernel(qn_ref, qr_ref, kn_ref, kr_ref, vt_ref, knx_ref, krx_ref, vtx_ref, o_ref, m_ref, l_ref, acc_ref):
    j = pl.program_id(1)

    @pl.when(j == 0)
    def _():
        _c_init(m_ref, l_ref, acc_ref)
        for hd in range(C_HEADS):
            _c_head_step(hd, qn_ref, qr_ref, knx_ref, krx_ref, vtx_ref, m_ref, l_ref, acc_ref)

    for hd in range(C_HEADS):
        _c_head_step(hd, qn_ref, qr_ref, kn_ref, kr_ref, vt_ref, m_ref, l_ref, acc_ref)

    @pl.when(j == pl.num_programs(1) - 1)
    def _():
        _c_finish(l_ref, acc_ref, o_ref)


def _c_attn_ctx_kernel(o_in_ref, qn_ref, qr_ref, knx_ref, krx_ref, vtx_ref, o_ref, m_ref, l_ref, acc_ref):
    del o_in_ref
    _c_init(m_ref, l_ref, acc_ref)
    for hd in range(C_HEADS):
        _c_head_step(hd, qn_ref, qr_ref, knx_ref, krx_ref, vtx_ref, m_ref, l_ref, acc_ref)
    _c_finish(l_ref, acc_ref, o_ref)


def _c_scratch(tq):
    return [pltpu.VMEM((C_HEADS, tq), F32), pltpu.VMEM((C_HEADS, tq), F32), pltpu.VMEM((C_HEADS, C_VDIM, tq), F32)]


def _c_attn(qn, qr, kn, kr, vt, n_lat, n_ctx, with_ctx_queries):
    tq = min(512, n_lat)
    tk = min(1024, n_lat)
    hw = C_HEADS * C_NOPE
    cb = n_lat // n_ctx
    qrow = lambda i, j: (i, 0)
    krow = lambda i, j: (j, 0)
    o = pl.pallas_call(
        _c_attn_lat_kernel,
        grid=(n_lat // tq, n_lat // tk),
        in_specs=[
            pl.BlockSpec((tq, hw), qrow),
            pl.BlockSpec((tq, C_QR_PAD), qrow),
            pl.BlockSpec((tk, hw), krow),
            pl.BlockSpec((tk, LANE), krow),
            pl.BlockSpec((hw, tk), lambda i, j: (0, j)),
            pl.BlockSpec((n_ctx, hw), lambda i, j: (cb, 0)),
            pl.BlockSpec((n_ctx, LANE), lambda i, j: (cb, 0)),
            pl.BlockSpec((hw, n_ctx), lambda i, j: (0, cb)),
        ],
        out_specs=pl.BlockSpec((tq, hw), qrow),
        out_shape=jax.ShapeDtypeStruct((n_lat + n_ctx, hw), BF16),
        scratch_shapes=_c_scratch(tq),
        compiler_params=_params("parallel", "arbitrary"),
        name="c_attn_lat",
    )(qn, qr, kn, kr, vt, kn, kr, vt)
    if not with_ctx_queries:
        return o
    return pl.pallas_call(
        _c_attn_ctx_kernel,
        grid=(1,),
        in_specs=[
            pl.BlockSpec(memory_space=pl.ANY),
            pl.BlockSpec((n_ctx, hw), lambda i: (cb, 0)),
            pl.BlockSpec((n_ctx, C_QR_PAD), lambda i: (cb, 0)),
            pl.BlockSpec((n_ctx, hw), lambda i: (cb, 0)),
            pl.BlockSpec((n_ctx, LANE), lambda i: (cb, 0)),
            pl.BlockSpec((hw, n_ctx), lambda i: (0, cb)),
        ],
        out_specs=pl.BlockSpec((n_ctx, hw), lambda i: (cb, 0)),
        out_shape=jax.ShapeDtypeStruct(o.shape, o.dtype),
        scratch_shapes=_c_scratch(n_ctx),
        input_output_aliases={0: 0},
        compiler_params=_params("arbitrary"),
        name="c_attn_ctx",
    )(o, qn, qr, kn, kr, vt)


def _rope_tables(n_lat, n_ctx, d_rot):
    n_rows = n_lat // GRID_W
    row = jnp.repeat(jnp.arange(n_rows), GRID_W).astype(F32)
    col = jnp.tile(jnp.arange(GRID_W), n_rows).astype(F32)
    d_axis = d_rot // 2
    inv = 1.0 / (ROPE_BASE ** (jnp.arange(0, d_axis, 2, dtype=F32) / d_axis))
    ar = row[:, None] * inv
    ac = col[:, None] * inv
    cos = jnp.concatenate([jnp.cos(ar), jnp.cos(ar), jnp.cos(ac), jnp.cos(ac)], axis=1)
    sin = jnp.concatenate([-jnp.sin(ar), jnp.sin(ar), -jnp.sin(ac), jnp.sin(ac)], axis=1)
    cos = jnp.pad(cos, ((0, 0), (0, LANE - d_rot)))
    sin = jnp.pad(sin, ((0, 0), (0, LANE - d_rot)))
    cos_ctx = jnp.pad(jnp.ones((n_ctx, d_rot), F32), ((0, 0), (0, LANE - d_rot)))
    return (jnp.concatenate([cos, cos_ctx], axis=0),
            jnp.concatenate([sin, jnp.zeros((n_ctx, LANE), F32)], axis=0))


def _pad_cols(w, width):
    return jnp.pad(w, ((0, 0), (0, width - w.shape[1])))


def kernel(x, c, ctx, c_ctx, ada_w, ada_b, norm_g, ffn_w_gu, ffn_w_down, a_w_qkv, a_sinks, a_w_o, b_w_in, b_conv_w, b_conv_b, b_lam, b_w_a, b_b_a, b_w_x, b_b_x, b_w_out, c_w_in, c_g_q, c_g_kv, c_w_uq, c_w_ukv, c_w_out):
    assert x.shape[0] == 1 and ctx.shape[0] == 1 and x.shape[2] == D_MODEL
    n_lat, n_ctx = x.shape[1], ctx.shape[1]
    assert n_ctx == ROW_TILE and n_lat % ROW_TILE == 0 and n_lat % GRID_W == 0
    n_lat_tiles = n_lat // ROW_TILE
    n_all_tiles = n_lat_tiles + 1

    xa = jnp.concatenate([x[0], ctx[0]], axis=0)
    s_rows = jnp.zeros((SUBLANE, D_MODEL), F32).at[0].set(c[0]).at[1].set(c_ctx)
    mods_all = _ada(s_rows, ada_w, ada_b)[:, :2].reshape(DEPTH, 2, N_MOD, D_MODEL)
    cos_a, sin_a = _rope_tables(n_lat, n_ctx, A_HEAD_DIM)
    cos_c, sin_c = _rope_tables(n_lat, n_ctx, C_ROPE)

    for i in range(DEPTH):
        last = i == DEPTH - 1
        kind, j = i % N_MIXERS, i // N_MIXERS
        mods = mods_all[i]
        g = norm_g[i]
        if kind == 0:
            q, k, v = _a_proj(xa, mods, g, a_w_qkv[j].astype(BF16), cos_a, sin_a, n_lat_tiles)
            o = _a_attn(q, k, v, a_sinks[j], n_lat, n_ctx, not last)
            w_o = a_w_o[j]
        elif kind == 1:
            xpre, gate = _b_proj(xa, mods, g, b_w_in[j].astype(BF16), n_lat_tiles)
            scan_args = lambda d: (b_conv_w[j], b_conv_b[j][None], b_lam[j, d][None], b_w_a[j, d].astype(BF16),
                                   b_b_a[j, d][None], b_w_x[j, d].astype(BF16), b_b_x[j, d][None], n_lat_tiles)
            hf = _b_scan(False, xpre, *scan_args(0))
            o = _b_scan(True, xpre, *scan_args(1), hf=hf, gate=gate)
            w_o = b_w_out[j]
        else:
            w_in = c_w_in[j]
            w_in_p = _pad_cols(w_in, C_IN_PAD).astype(BF16)
            w_uq3 = c_w_uq[j].reshape(C_Q_RANK, C_HEADS, C_NOPE + C_ROPE)
            w_uq_p = jnp.concatenate(
                [w_uq3[..., :C_NOPE].reshape(C_Q_RANK, C_HEADS * C_NOPE),
                 jnp.pad(w_uq3[..., C_NOPE:], ((0, 0), (0, 0), (0, LANE - C_ROPE))).reshape(C_Q_RANK, C_QR_PAD)],
                axis=1).astype(BF16)
            w_ukv3 = c_w_ukv[j].reshape(C_KV_RANK, C_HEADS, C_NOPE + C_VDIM)
            w_uk = w_ukv3[..., :C_NOPE].reshape(C_KV_RANK, C_HEADS * C_NOPE).astype(BF16)
            w_uvt = w_ukv3[..., C_NOPE:].reshape(C_KV_RANK, C_HEADS * C_VDIM).T.astype(BF16)
            qn, qr, kn, kr, vt = _c_proj(xa, mods, g, w_in_p, c_g_q[j][None], c_g_kv[j][None], w_uq_p, w_uk, w_uvt,
                                         cos_c, sin_c, n_lat_tiles)
            o = _c_attn(qn, qr, kn, kr, vt, n_lat, n_ctx, not last)
            w_o = c_w_out[j]
        xa = _post(o, xa, mods, g, w_o.astype(BF16), ffn_w_gu[i].astype(BF16), ffn_w_down[i].astype(BF16),
                   n_lat_tiles, n_lat_tiles if last else n_all_tiles)
    return xa[None, :n_lat]
```

```python
import functools
import math

import jax
import jax.numpy as jnp
from jax import lax
from jax.experimental import pallas as pl
from jax.experimental.pallas import tpu as pltpu

D_MODEL = 1024
DEPTH = 4
GRID_W = 64
N_MIXERS = 3
BLOCK = 128
ROPE_BASE = 10000.0
RMS_EPS = 1e-6
N_MOD = 6

A_HEADS = 8
A_KV_HEADS = 2
A_GROUP = A_HEADS // A_KV_HEADS
A_HEAD_DIM = D_MODEL // A_HEADS
A_WINDOW = 128
A_SCALE = A_HEAD_DIM ** -0.5
A_Q_W = A_HEADS * A_HEAD_DIM
A_KV_W = A_KV_HEADS * A_HEAD_DIM
A_QKV = A_Q_W + 2 * A_KV_W

B_WIDTH = D_MODEL
B_BLOCKS = 4
B_BLOCK_W = B_WIDTH // B_BLOCKS
B_CONV_W = 4
B_LRU_C = 8.0

C_HEADS = 8
C_NOPE = 128
C_ROPE = 64
C_VDIM = 128
C_Q_RANK = D_MODEL // 2
C_KV_RANK = D_MODEL // 4
C_SCALE = (C_NOPE + C_ROPE) ** -0.5
C_QSCALE = C_SCALE * math.log2(math.e)

FFN_HIDDEN = ((8 * D_MODEL + 3 * 256 - 1) // (3 * 256)) * 256

LANE = 128
SUBLANE = 8
ROW_TILE = 256
FFN_CHUNK = 256
C_SAFE_SUM = 2.0 ** 60
VMEM_LIMIT = 56 * 1024 * 1024
NEG = -1e30

BF16 = jnp.bfloat16
F32 = jnp.float32


def _dot(a, b):
    return jnp.dot(a, b, preferred_element_type=F32)


def _dot_nt(a, b):
    return lax.dot_general(a, b, (((1,), (1,)), ((), ())), preferred_element_type=F32)


def _resident(shape):
    nd = len(shape)
    return pl.BlockSpec(shape, lambda *_: (0,) * nd, pipeline_mode=pl.Buffered(1))


def _params(*sem):
    return pltpu.CompilerParams(dimension_semantics=sem, vmem_limit_bytes=VMEM_LIMIT)


def _rms(x, g):
    ms = jnp.mean(x * x, axis=-1, keepdims=True)
    return x * lax.rsqrt(ms + RMS_EPS) * g


def _norm_mod(x, g, shift, scale):
    return _rms(x, g) * (1.0 + scale) + shift


def _rope(z, cos, sin, half):
    lane = lax.broadcasted_iota(jnp.int32, z.shape, 1)
    first = (lane % (2 * half)) < half
    partner = jnp.where(first, pltpu.roll(z, LANE - half, 1), pltpu.roll(z, half, 1))
    return z * cos + partner * sin


def _ada_kernel(s_ref, w_ref, b_ref, o_ref):
    s = s_ref[...]
    s = s * jax.nn.sigmoid(s)
    o_ref[0] = _dot(s.astype(BF16), w_ref[0].astype(BF16)) + b_ref[0]


def _ada(s_rows, ada_w, ada_b):
    tn = 1536
    depth, d, n = ada_w.shape
    return pl.pallas_call(
        _ada_kernel,
        grid=(depth, n // tn),
        in_specs=[
            pl.BlockSpec((SUBLANE, d), lambda l, j: (0, 0)),
            pl.BlockSpec((1, d, tn), lambda l, j: (l, 0, j)),
            pl.BlockSpec((1, 1, tn), lambda l, j: (l, 0, j)),
        ],
        out_specs=pl.BlockSpec((1, SUBLANE, tn), lambda l, j: (l, 0, j)),
        out_shape=jax.ShapeDtypeStruct((depth, SUBLANE, n), F32),
        compiler_params=_params("parallel", "parallel"),
        name="ada",
    )(s_rows, ada_w, ada_b.reshape(depth, 1, n))


def _mods_spec(n_lat_tiles):
    return pl.BlockSpec((1, N_MOD, D_MODEL), lambda i: (jnp.where(i >= n_lat_tiles, 1, 0), 0, 0))


def _post_kernel(o_ref, x_ref, mods_ref, g_ref, wo_ref, wgu_ref, wd_ref, out_ref, act_ref):
    m = mods_ref[0]
    g = g_ref[...]
    y = _dot(o_ref[...], wo_ref[...])
    x1 = x_ref[...] + m[2:3] * _rms(y, g[1:2])
    h = _norm_mod(x1, g[2:3], m[3:4], m[4:5]).astype(BF16)
    for c in range(FFN_HIDDEN // FFN_CHUNK):
        lo = c * FFN_CHUNK
        zg = _dot(h, wgu_ref[:, lo:lo + FFN_CHUNK])
        zu = _dot(h, wgu_ref[:, FFN_HIDDEN + lo:FFN_HIDDEN + lo + FFN_CHUNK])
        act_ref[:, lo:lo + FFN_CHUNK] = (zg * jax.nn.sigmoid(zg) * zu).astype(BF16)
    f = _dot(act_ref[...], wd_ref[...])
    out_ref[...] = x1 + m[5:6] * _rms(f, g[3:4])


def _post(o, xa, mods, g, w_o, w_gu, w_down, n_lat_tiles, n_tiles):
    d = D_MODEL
    row = lambda i: (i, 0)
    return pl.pallas_call(
        _post_kernel,
        grid=(n_tiles,),
        in_specs=[
            pl.BlockSpec((ROW_TILE, w_o.shape[0]), row),
            pl.BlockSpec((ROW_TILE, d), row),
            _mods_spec(n_lat_tiles),
            _resident(g.shape),
            _resident(w_o.shape),
            _resident(w_gu.shape),
            _resident(w_down.shape),
        ],
        out_specs=pl.BlockSpec((ROW_TILE, d), row),
        out_shape=jax.ShapeDtypeStruct((n_tiles * ROW_TILE, d), F32),
        scratch_shapes=[pltpu.VMEM((ROW_TILE, FFN_HIDDEN), BF16)],
        compiler_params=_params("parallel"),
        name="post_ffn",
    )(o, xa, mods, g, w_o, w_gu, w_down)


def _a_proj_kernel(x_ref, mods_ref, g_ref, w_ref, cos_ref, sin_ref, q_ref, k_ref, v_ref):
    m = mods_ref[0]
    h = _norm_mod(x_ref[...], g_ref[0:1], m[0:1], m[1:2]).astype(BF16)
    z = _dot(h, w_ref[...])
    cos = cos_ref[...]
    sin = sin_ref[...]
    half = A_HEAD_DIM // 4
    for hd in range(A_HEADS):
        lo = hd * A_HEAD_DIM
        q_ref[:, lo:lo + A_HEAD_DIM] = _rope(z[:, lo:lo + A_HEAD_DIM], cos, sin, half).astype(BF16)
    for hd in range(A_KV_HEADS):
        lo = hd * A_HEAD_DIM
        k_ref[:, lo:lo + A_HEAD_DIM] = _rope(z[:, A_Q_W + lo:A_Q_W + lo + A_HEAD_DIM], cos, sin, half).astype(BF16)
    v_ref[...] = z[:, A_Q_W + A_KV_W:].astype(BF16)


def _a_proj(xa, mods, g, w_qkv, cos, sin, n_lat_tiles):
    n_tiles = xa.shape[0] // ROW_TILE
    row = lambda i: (i, 0)
    return pl.pallas_call(
        _a_proj_kernel,
        grid=(n_tiles,),
        in_specs=[
            pl.BlockSpec((ROW_TILE, D_MODEL), row),
            _mods_spec(n_lat_tiles),
            _resident(g.shape),
            _resident(w_qkv.shape),
            pl.BlockSpec((ROW_TILE, LANE), row),
            pl.BlockSpec((ROW_TILE, LANE), row),
        ],
        out_specs=[
            pl.BlockSpec((ROW_TILE, A_Q_W), row),
            pl.BlockSpec((ROW_TILE, A_KV_W), row),
            pl.BlockSpec((ROW_TILE, A_KV_W), row),
        ],
        out_shape=[
            jax.ShapeDtypeStruct((xa.shape[0], A_Q_W), BF16),
            jax.ShapeDtypeStruct((xa.shape[0], A_KV_W), BF16),
            jax.ShapeDtypeStruct((xa.shape[0], A_KV_W), BF16),
        ],
        compiler_params=_params("parallel"),
        name="a_proj",
    )(xa, mods, g, w_qkv, cos, sin)


def _a_attn_kernel(nb, sink_ref, q_ref, kp_ref, kc_ref, kn_ref, kx_ref, vp_ref, vc_ref, vn_ref, vx_ref, o_ref):
    i = pl.program_id(0)
    is_lat = i < nb
    prev_ok = jnp.logical_and(is_lat, i >= 1)
    next_ok = i < nb - 1
    rows = A_GROUP * BLOCK
    r = lax.broadcasted_iota(jnp.int32, (rows, BLOCK), 0) % BLOCK
    c = lax.broadcasted_iota(jnp.int32, (rows, BLOCK), 1)
    mask_p = jnp.logical_and(c >= r, prev_ok)
    mask_n = jnp.logical_and(c <= r, next_ok)
    grp = lax.broadcasted_iota(jnp.int32, (rows, 1), 0) // BLOCK
    for kh in range(A_KV_HEADS):
        ks = slice(kh * A_HEAD_DIM, (kh + 1) * A_HEAD_DIM)
        qg = jnp.concatenate(
            [q_ref[:, (kh * A_GROUP + gi) * A_HEAD_DIM:(kh * A_GROUP + gi + 1) * A_HEAD_DIM] for gi in range(A_GROUP)],
            axis=0)
        sink = jnp.zeros((rows, 1), F32)
        for gi in range(A_GROUP):
            sink = jnp.where(grp == gi, sink_ref[kh * A_GROUP + gi], sink)
        s_x = _dot_nt(qg, kx_ref[:, ks]) * A_SCALE
        s_p = jnp.where(mask_p, _dot_nt(qg, kp_ref[:, ks]) * A_SCALE, NEG)
        s_c = jnp.where(is_lat, _dot_nt(qg, kc_ref[:, ks]) * A_SCALE, NEG)
        s_n = jnp.where(mask_n, _dot_nt(qg, kn_ref[:, ks]) * A_SCALE, NEG)
        mx = jnp.maximum(jnp.max(s_x, axis=-1, keepdims=True), jnp.max(s_c, axis=-1, keepdims=True))
        mx = jnp.maximum(mx, jnp.maximum(jnp.max(s_p, axis=-1, keepdims=True), jnp.max(s_n, axis=-1, keepdims=True)))
        mx = jnp.maximum(mx, sink)
        e_x = jnp.exp(s_x - mx)
        e_p = jnp.exp(s_p - mx)
        e_c = jnp.exp(s_c - mx)
        e_n = jnp.exp(s_n - mx)
        den = (jnp.exp(sink - mx) + jnp.sum(e_x, axis=-1, keepdims=True) + jnp.sum(e_p, axis=-1, keepdims=True)
               + jnp.sum(e_c, axis=-1, keepdims=True) + jnp.sum(e_n, axis=-1, keepdims=True))
        o = (_dot(e_x.astype(BF16), vx_ref[:, ks]) + _dot(e_p.astype(BF16), vp_ref[:, ks])
             + _dot(e_c.astype(BF16), vc_ref[:, ks]) + _dot(e_n.astype(BF16), vn_ref[:, ks]))
        o = o / den
        for gi in range(A_GROUP):
            hd = kh * A_GROUP + gi
            o_ref[:, hd * A_HEAD_DIM:(hd + 1) * A_HEAD_DIM] = o[gi * BLOCK:(gi + 1) * BLOCK].astype(BF16)


def _a_attn(q, k, v, sinks, n_lat, n_ctx, with_ctx_queries):
    nb = n_lat // BLOCK
    n_blocks = nb + (n_ctx // BLOCK if with_ctx_queries else 0)
    last = (n_lat + n_ctx) // BLOCK - 1
    cur = lambda i: (i, 0)
    prev = lambda i: (jnp.maximum(i - 1, 0), 0)
    nxt = lambda i: (jnp.minimum(i + 1, last), 0)
    ctx = lambda i: (n_lat // n_ctx, 0)
    kv_specs = [
        pl.BlockSpec((BLOCK, A_KV_W), prev),
        pl.BlockSpec((BLOCK, A_KV_W), cur),
        pl.BlockSpec((BLOCK, A_KV_W), nxt),
        pl.BlockSpec((n_ctx, A_KV_W), ctx),
    ]
    return pl.pallas_call(
        functools.partial(_a_attn_kernel, nb),
        grid=(n_blocks,),
        in_specs=[pl.BlockSpec(memory_space=pltpu.SMEM), pl.BlockSpec((BLOCK, A_Q_W), cur)] + kv_specs + kv_specs,
        out_specs=pl.BlockSpec((BLOCK, A_Q_W), cur),
        out_shape=jax.ShapeDtypeStruct((n_blocks * BLOCK, A_Q_W), BF16),
        compiler_params=_params("parallel"),
        name="a_attn",
    )(sinks, q, k, k, k, k, v, v, v, v)


def _b_proj_kernel(x_ref, mods_ref, g_ref, w_ref, xpre_ref, gate_ref):
    m = mods_ref[0]
    h = _norm_mod(x_ref[...], g_ref[0:1], m[0:1], m[1:2]).astype(BF16)
    xpre_ref[...] = _dot(h, w_ref[:, :B_WIDTH])
    gate_ref[...] = jax.nn.gelu(_dot(h, w_ref[:, B_WIDTH:]), approximate=True)


def _b_proj(xa, mods, g, w_in, n_lat_tiles):
    n_tiles = xa.shape[0] // ROW_TILE
    row = lambda i: (i, 0)
    return pl.pallas_call(
        _b_proj_kernel,
        grid=(n_tiles,),
        in_specs=[
            pl.BlockSpec((ROW_TILE, D_MODEL), row),
            _mods_spec(n_lat_tiles),
            _resident(g.shape),
            _resident(w_in.shape),
        ],
        out_specs=[pl.BlockSpec((ROW_TILE, B_WIDTH), row), pl.BlockSpec((ROW_TILE, B_WIDTH), row)],
        out_shape=[jax.ShapeDtypeStruct((xa.shape[0], B_WIDTH), F32)] * 2,
        compiler_params=_params("parallel"),
        name="b_proj",
    )(xa, mods, g, w_in)


def _softplus(y):
    return jnp.maximum(y, 0.0) + jnp.log1p(jnp.exp(-jnp.abs(y)))


def _expm1(x, u):
    one = u == 1.0
    k = (u - 1.0) * x / jnp.where(one, 1.0, jnp.log(u))
    return jnp.where(one, x, jnp.where(x < -1.0, u - 1.0, k))


def _b_tile_index(reverse, n_lat_tiles, i):
    lat = (n_lat_tiles - i) if reverse else (i - 1)
    return jnp.where(i == 0, n_lat_tiles, lat)


def _b_scan_kernel(reverse, n_lat_tiles, *refs):
    if reverse:
        (xc_ref, xp_ref, xn_ref, cw_ref, cb_ref, lam_ref, wa_ref, ba_ref, wx_ref, bx_ref,
         hf_ref, gate_ref, out_ref, carry_ref) = refs
    else:
        (xc_ref, xp_ref, xn_ref, cw_ref, cb_ref, lam_ref, wa_ref, ba_ref, wx_ref, bx_ref,
         out_ref, carry_ref) = refs
    i = pl.program_id(0)
    tile = _b_tile_index(reverse, n_lat_tiles, i)

    @pl.when(i == 0)
    def _():
        carry_ref[...] = jnp.zeros(carry_ref.shape, F32)

    prev_ok = jnp.logical_and(tile >= 1, tile <= n_lat_tiles - 1)
    next_ok = tile <= n_lat_tiles - 2
    u = xc_ref[...]
    n = u.shape[0]
    row = lax.broadcasted_iota(jnp.int32, (n, 1), 0)
    pm = jnp.where(prev_ok, xp_ref[SUBLANE - 1:SUBLANE, :], 0.0)
    n0 = jnp.where(next_ok, xn_ref[0:1, :], 0.0)
    n1 = jnp.where(next_ok, xn_ref[1:2, :], 0.0)
    um1 = jnp.where(row == 0, pm, pltpu.roll(u, 1, 0))
    up1 = jnp.where(row == n - 1, n0, pltpu.roll(u, n - 1, 0))
    up2 = jnp.where(row == n - 2, n0, jnp.where(row == n - 1, n1, pltpu.roll(u, n - 2, 0)))
    cw = cw_ref[...]
    xb = cb_ref[...] + (cw[0:1] * um1 + cw[1:2] * u + cw[2:3] * up1 + cw[3:4] * up2)

    xb16 = xb.astype(BF16)
    rs, gs = [], []
    for nblk in range(B_BLOCKS):
        sl = slice(nblk * B_BLOCK_W, (nblk + 1) * B_BLOCK_W)
        rs.append(_dot(xb16[:, sl], wa_ref[nblk]))
        gs.append(_dot(xb16[:, sl], wx_ref[nblk]))
    r = jax.nn.sigmoid(jnp.concatenate(rs, axis=1) + ba_ref[...])
    gi = jax.nn.sigmoid(jnp.concatenate(gs, axis=1) + bx_ref[...])
    log_a = (-B_LRU_C * r) * _softplus(-lam_ref[...])
    a = jnp.exp(log_a)
    b = jnp.sqrt(-_expm1(2.0 * log_a, a * a)) * (gi * xb)

    sub = row % SUBLANE
    s = 1
    while s < SUBLANE:
        if reverse:
            shift, valid = n - s, sub < SUBLANE - s
        else:
            shift, valid = s, sub >= s
        a_sh = pltpu.roll(a, shift, 0)
        b_sh = pltpu.roll(b, shift, 0)
        b = jnp.where(valid, a * b_sh + b, b)
        a = jnp.where(valid, a * a_sh, a)
        s *= 2
    carry = carry_ref[0:1, :]
    n_groups = n // SUBLANE
    hs = [None] * n_groups
    for k in range(n_groups):
        gidx = n_groups - 1 - k if reverse else k
        rows = slice(gidx * SUBLANE, (gidx + 1) * SUBLANE)
        hg = a[rows] * carry + b[rows]
        carry = hg[0:1, :] if reverse else hg[SUBLANE - 1:SUBLANE, :]
        hs[gidx] = hg
    carry_ref[0:1, :] = carry
    h = jnp.concatenate(hs, axis=0)
    if reverse:
        out_ref[...] = ((hf_ref[...] + h) * gate_ref[...]).astype(out_ref.dtype)
    else:
        out_ref[...] = h


def _b_scan(reverse, xpre, conv_w, conv_b, lam, w_a, b_a, w_x, b_x, n_lat_tiles, hf=None, gate=None):
    n_tiles = xpre.shape[0] // ROW_TILE
    per8 = ROW_TILE // SUBLANE
    last8 = xpre.shape[0] // SUBLANE - 1
    tile = functools.partial(_b_tile_index, reverse, n_lat_tiles)
    cur = lambda i: (tile(i), 0)
    prev8 = lambda i: (jnp.maximum(tile(i) * per8 - 1, 0), 0)
    next8 = lambda i: (jnp.minimum((tile(i) + 1) * per8, last8), 0)
    w = B_WIDTH
    in_specs = [
        pl.BlockSpec((ROW_TILE, w), cur),
        pl.BlockSpec((SUBLANE, w), prev8),
        pl.BlockSpec((SUBLANE, w), next8),
        _resident(conv_w.shape), _resident(conv_b.shape), _resident(lam.shape),
        _resident(w_a.shape), _resident(b_a.shape), _resident(w_x.shape), _resident(b_x.shape),
    ]
    args = [xpre, xpre, xpre, conv_w, conv_b, lam, w_a, b_a, w_x, b_x]
    if reverse:
        in_specs += [pl.BlockSpec((ROW_TILE, w), cur), pl.BlockSpec((ROW_TILE, w), cur)]
        args += [hf, gate]
    return pl.pallas_call(
        functools.partial(_b_scan_kernel, reverse, n_lat_tiles),
        grid=(n_tiles,),
        in_specs=in_specs,
        out_specs=pl.BlockSpec((ROW_TILE, w), cur),
        out_shape=jax.ShapeDtypeStruct(xpre.shape, BF16 if reverse else F32),
        scratch_shapes=[pltpu.VMEM((SUBLANE, w), F32)],
        compiler_params=_params("arbitrary"),
        name="b_scan_rev" if reverse else "b_scan_fwd",
    )(*args)


C_IN_PAD = C_Q_RANK + C_KV_RANK + LANE
C_QR_PAD = C_HEADS * LANE


def _c_proj_kernel(x_ref, mods_ref, g_ref, win_ref, gq_ref, gkv_ref, wuq_ref, wuk_ref, wuvt_ref, cos_ref, sin_ref,
                   qn_ref, qr_ref, kn_ref, kr_ref, vt_ref):
    m = mods_ref[0]
    h = _norm_mod(x_ref[...], g_ref[0:1], m[0:1], m[1:2]).astype(BF16)
    z = _dot(h, win_ref[...])
    cq = _rms(z[:, :C_Q_RANK], gq_ref[...]).astype(BF16)
    ckv = _rms(z[:, C_Q_RANK:C_Q_RANK + C_KV_RANK], gkv_ref[...]).astype(BF16)
    cos = cos_ref[...]
    sin = sin_ref[...]
    half = C_ROPE // 4
    kr_ref[...] = _rope(z[:, C_Q_RANK + C_KV_RANK:], cos, sin, half).astype(BF16)
    q = _dot(cq, wuq_ref[...]) * C_QSCALE
    qn_ref[...] = q[:, :C_HEADS * C_NOPE].astype(BF16)
    for hd in range(C_HEADS):
        lo = C_HEADS * C_NOPE + hd * LANE
        qr_ref[:, hd * LANE:(hd + 1) * LANE] = _rope(q[:, lo:lo + LANE], cos, sin, half).astype(BF16)
    kn_ref[...] = _dot(ckv, wuk_ref[...]).astype(BF16)
    vt_ref[...] = _dot_nt(wuvt_ref[...], ckv).astype(BF16)


def _c_proj(xa, mods, g, w_in, g_q, g_kv, w_uq, w_uk, w_uvt, cos, sin, n_lat_tiles):
    n_rows = xa.shape[0]
    n_tiles = n_rows // ROW_TILE
    row = lambda i: (i, 0)
    hw = C_HEADS * C_NOPE
    return pl.pallas_call(
        _c_proj_kernel,
        grid=(n_tiles,),
        in_specs=[
            pl.BlockSpec((ROW_TILE, D_MODEL), row),
            _mods_spec(n_lat_tiles),
            _resident(g.shape), _resident(w_in.shape), _resident(g_q.shape), _resident(g_kv.shape),
            _resident(w_uq.shape), _resident(w_uk.shape), _resident(w_uvt.shape),
            pl.BlockSpec((ROW_TILE, LANE), row),
            pl.BlockSpec((ROW_TILE, LANE), row),
        ],
        out_specs=[
            pl.BlockSpec((ROW_TILE, hw), row),
            pl.BlockSpec((ROW_TILE, C_QR_PAD), row),
            pl.BlockSpec((ROW_TILE, hw), row),
            pl.BlockSpec((ROW_TILE, LANE), row),
            pl.BlockSpec((C_HEADS * C_VDIM, ROW_TILE), lambda i: (0, i)),
        ],
        out_shape=[
            jax.ShapeDtypeStruct((n_rows, hw), BF16),
            jax.ShapeDtypeStruct((n_rows, C_QR_PAD), BF16),
            jax.ShapeDtypeStruct((n_rows, hw), BF16),
            jax.ShapeDtypeStruct((n_rows, LANE), BF16),
            jax.ShapeDtypeStruct((C_HEADS * C_VDIM, n_rows), BF16),
        ],
        compiler_params=_params("parallel"),
        name="c_proj",
    )(xa, mods, g, w_in, g_q, g_kv, w_uq, w_uk, w_uvt, cos, sin)


def _c_scores(hd, qn_ref, qr_ref, kn_ref, kr_ref):
    hs = slice(hd * LANE, (hd + 1) * LANE)
    qcat = jnp.concatenate([qn_ref[:, hs], qr_ref[:, hs]], axis=1)
    kcat = jnp.concatenate([kn_ref[:, hs], kr_ref[...]], axis=1)
    return _dot_nt(kcat, qcat)


def _c_key_block_exact(qn_ref, qr_ref, kn_ref, kr_ref, vt_ref, m_ref, l_ref, acc_ref):
    for hd in range(C_HEADS):
        hs = slice(hd * LANE, (hd + 1) * LANE)
        st = _c_scores(hd, qn_ref, qr_ref, kn_ref, kr_ref)
        m_old = m_ref[hd:hd + 1, :]
        m_new = jnp.maximum(m_old, jnp.max(st, axis=0, keepdims=True))
        alpha = jnp.exp2(m_old - m_new)
        p = jnp.exp2(st - m_new)
        l_ref[hd:hd + 1, :] = alpha * l_ref[hd:hd + 1, :] + jnp.sum(p, axis=0, keepdims=True)
        acc_ref[hd] = alpha * acc_ref[hd] + _dot(vt_ref[hs, :], p.astype(BF16))
        m_ref[hd:hd + 1, :] = m_new


def _c_key_block_fixed_ref(qn_ref, qr_ref, kn_ref, kr_ref, vt_ref, m_ref, ps_ref, pv_ref):
    for hd in range(C_HEADS):
        hs = slice(hd * LANE, (hd + 1) * LANE)
        p = jnp.exp2(_c_scores(hd, qn_ref, qr_ref, kn_ref, kr_ref) - m_ref[hd:hd + 1, :])
        ps_ref[hd:hd + 1, :] = jnp.sum(p, axis=0, keepdims=True)
        pv_ref[hd] = _dot(vt_ref[hs, :], p.astype(BF16))


def _c_init(m_ref, l_ref, acc_ref):
    m_ref[...] = jnp.full(m_ref.shape, NEG, F32)
    l_ref[...] = jnp.zeros(l_ref.shape, F32)
    acc_ref[...] = jnp.zeros(acc_ref.shape, F32)


def _c_finish(l_ref, acc_ref, o_ref):
    for hd in range(C_HEADS):
        ot = acc_ref[hd] / l_ref[hd:hd + 1, :]
        o_ref[:, hd * C_VDIM:(hd + 1) * C_VDIM] = ot.T.astype(o_ref.dtype)


def _c_attn_lat_kernel(qn_ref, qr_ref, kn_ref, kr_ref, vt_ref, knx_ref, krx_ref, vtx_ref, o_ref,
                       m_ref, l_ref, acc_ref, ps_ref, pv_ref):
    j = pl.program_id(1)

    @pl.when(j == 0)
    def _():
        _c_init(m_ref, l_ref, acc_ref)
        _c_key_block_exact(qn_ref, qr_ref, knx_ref, krx_ref, vtx_ref, m_ref, l_ref, acc_ref)

    _c_key_block_fixed_ref(qn_ref, qr_ref, kn_ref, kr_ref, vt_ref, m_ref, ps_ref, pv_ref)
    safe = jnp.max(ps_ref[...]) < C_SAFE_SUM

    @pl.when(safe)
    def _():
        l_ref[...] = l_ref[...] + ps_ref[...]
        acc_ref[...] = acc_ref[...] + pv_ref[...]

    @pl.when(jnp.logical_not(safe))
    def _():
        _c_key_block_exact(qn_ref, qr_ref, kn_ref, kr_ref, vt_ref, m_ref, l_ref, acc_ref)

    @pl.when(j == pl.num_programs(1) - 1)
    def _():
        _c_finish(l_ref, acc_ref, o_ref)


def _c_attn_ctx_kernel(o_in_ref, qn_ref, qr_ref, knx_ref, krx_ref, vtx_ref, o_ref, m_ref, l_ref, acc_ref):
    del o_in_ref
    _c_init(m_ref, l_ref, acc_ref)
    _c_key_block_exact(qn_ref, qr_ref, knx_ref, krx_ref, vtx_ref, m_ref, l_ref, acc_ref)
    _c_finish(l_ref, acc_ref, o_ref)


def _c_scratch(tq, with_uncommitted):
    stats = [pltpu.VMEM((C_HEADS, tq), F32), pltpu.VMEM((C_HEADS, tq), F32), pltpu.VMEM((C_HEADS, C_VDIM, tq), F32)]
    return stats + (stats[1:] if with_uncommitted else [])


def _c_attn(qn, qr, kn, kr, vt, n_lat, n_ctx, with_ctx_queries):
    tq = min(512, n_lat)
    tk = min(2048, n_lat)
    hw = C_HEADS * C_NOPE
    cb = n_lat // n_ctx
    qrow = lambda i, j: (i, 0)
    krow = lambda i, j: (j, 0)
    o = pl.pallas_call(
        _c_attn_lat_kernel,
        grid=(n_lat // tq, n_lat // tk),
        in_specs=[
            pl.BlockSpec((tq, hw), qrow),
            pl.BlockSpec((tq, C_QR_PAD), qrow),
            pl.BlockSpec((tk, hw), krow),
            pl.BlockSpec((tk, LANE), krow),
            pl.BlockSpec((hw, tk), lambda i, j: (0, j)),
            pl.BlockSpec((n_ctx, hw), lambda i, j: (cb, 0)),
            pl.BlockSpec((n_ctx, LANE), lambda i, j: (cb, 0)),
            pl.BlockSpec((hw, n_ctx), lambda i, j: (0, cb)),
        ],
        out_specs=pl.BlockSpec((tq, hw), qrow),
        out_shape=jax.ShapeDtypeStruct((n_lat + n_ctx, hw), BF16),
        scratch_shapes=_c_scratch(tq, True),
        compiler_params=_params("parallel", "arbitrary"),
        name="c_attn_lat",
    )(qn, qr, kn, kr, vt, kn, kr, vt)
    if not with_ctx_queries:
        return o
    return pl.pallas_call(
        _c_attn_ctx_kernel,
        grid=(1,),
        in_specs=[
            pl.BlockSpec(memory_space=pl.ANY),
            pl.BlockSpec((n_ctx, hw), lambda i: (cb, 0)),
            pl.BlockSpec((n_ctx, C_QR_PAD), lambda i: (cb, 0)),
            pl.BlockSpec((n_ctx, hw), lambda i: (cb, 0)),
            pl.BlockSpec((n_ctx, LANE), lambda i: (cb, 0)),
            pl.BlockSpec((hw, n_ctx), lambda i: (0, cb)),
        ],
        out_specs=pl.BlockSpec((n_ctx, hw), lambda i: (cb, 0)),
        out_shape=jax.ShapeDtypeStruct(o.shape, o.dtype),
        scratch_shapes=_c_scratch(n_ctx, False),
        input_output_aliases={0: 0},
        compiler_params=_params("arbitrary"),
        name="c_attn_ctx",
    )(o, qn, qr, kn, kr, vt)


def _rope_tables(n_lat, n_ctx, d_rot):
    n_rows = n_lat // GRID_W
    row = jnp.repeat(jnp.arange(n_rows), GRID_W).astype(F32)
    col = jnp.tile(jnp.arange(GRID_W), n_rows).astype(F32)
    d_axis = d_rot // 2
    inv = 1.0 / (ROPE_BASE ** (jnp.arange(0, d_axis, 2, dtype=F32) / d_axis))
    ar = row[:, None] * inv
    ac = col[:, None] * inv
    cos = jnp.concatenate([jnp.cos(ar), jnp.cos(ar), jnp.cos(ac), jnp.cos(ac)], axis=1)
    sin = jnp.concatenate([-jnp.sin(ar), jnp.sin(ar), -jnp.sin(ac), jnp.sin(ac)], axis=1)
    cos = jnp.pad(cos, ((0, 0), (0, LANE - d_rot)))
    sin = jnp.pad(sin, ((0, 0), (0, LANE - d_rot)))
    cos_ctx = jnp.pad(jnp.ones((n_ctx, d_rot), F32), ((0, 0), (0, LANE - d_rot)))
    return (jnp.concatenate([cos, cos_ctx], axis=0),
            jnp.concatenate([sin, jnp.zeros((n_ctx, LANE), F32)], axis=0))


def _pad_cols(w, width):
    return jnp.pad(w, ((0, 0), (0, width - w.shape[1])))


def kernel(x, c, ctx, c_ctx, ada_w, ada_b, norm_g, ffn_w_gu, ffn_w_down, a_w_qkv, a_sinks, a_w_o, b_w_in, b_conv_w, b_conv_b, b_lam, b_w_a, b_b_a, b_w_x, b_b_x, b_w_out, c_w_in, c_g_q, c_g_kv, c_w_uq, c_w_ukv, c_w_out):
    assert x.shape[0] == 1 and ctx.shape[0] == 1 and x.shape[2] == D_MODEL
    n_lat, n_ctx = x.shape[1], ctx.shape[1]
    assert n_ctx == ROW_TILE and n_lat % ROW_TILE == 0 and n_lat % GRID_W == 0
    n_lat_tiles = n_lat // ROW_TILE
    n_all_tiles = n_lat_tiles + 1

    xa = jnp.concatenate([x[0], ctx[0]], axis=0)
    s_rows = jnp.zeros((SUBLANE, D_MODEL), F32).at[0].set(c[0]).at[1].set(c_ctx)
    mods_all = _ada(s_rows, ada_w, ada_b)[:, :2].reshape(DEPTH, 2, N_MOD, D_MODEL)
    cos_a, sin_a = _rope_tables(n_lat, n_ctx, A_HEAD_DIM)
    cos_c, sin_c = _rope_tables(n_lat, n_ctx, C_ROPE)

    for i in range(DEPTH):
        last = i == DEPTH - 1
        kind, j = i % N_MIXERS, i // N_MIXERS
        mods = mods_all[i]
        g = norm_g[i]
        if kind == 0:
            q, k, v = _a_proj(xa, mods, g, a_w_qkv[j].astype(BF16), cos_a, sin_a, n_lat_tiles)
            o = _a_attn(q, k, v, a_sinks[j], n_lat, n_ctx, not last)
            w_o = a_w_o[j]
        elif kind == 1:
            xpre, gate = _b_proj(xa, mods, g, b_w_in[j].astype(BF16), n_lat_tiles)
            scan_args = lambda d: (b_conv_w[j], b_conv_b[j][None], b_lam[j, d][None], b_w_a[j, d].astype(BF16),
                                   b_b_a[j, d][None], b_w_x[j, d].astype(BF16), b_b_x[j, d][None], n_lat_tiles)
            hf = _b_scan(False, xpre, *scan_args(0))
            o = _b_scan(True, xpre, *scan_args(1), hf=hf, gate=gate)
            w_o = b_w_out[j]
        else:
            w_in = c_w_in[j]
            w_in_p = _pad_cols(w_in, C_IN_PAD).astype(BF16)
            w_uq3 = c_w_uq[j].reshape(C_Q_RANK, C_HEADS, C_NOPE + C_ROPE)
            w_uq_p = jnp.concatenate(
                [w_uq3[..., :C_NOPE].reshape(C_Q_RANK, C_HEADS * C_NOPE),
                 jnp.pad(w_uq3[..., C_NOPE:], ((0, 0), (0, 0), (0, LANE - C_ROPE))).reshape(C_Q_RANK, C_QR_PAD)],
                axis=1).astype(BF16)
            w_ukv3 = c_w_ukv[j].reshape(C_KV_RANK, C_HEADS, C_NOPE + C_VDIM)
            w_uk = w_ukv3[..., :C_NOPE].reshape(C_KV_RANK, C_HEADS * C_NOPE).astype(BF16)
            w_uvt = w_ukv3[..., C_NOPE:].reshape(C_KV_RANK, C_HEADS * C_VDIM).T.astype(BF16)
            qn, qr, kn, kr, vt = _c_proj(xa, mods, g, w_in_p, c_g_q[j][None], c_g_kv[j][None], w_uq_p, w_uk, w_uvt,
                                         cos_c, sin_c, n_lat_tiles)
            o = _c_attn(qn, qr, kn, kr, vt, n_lat, n_ctx, not last)
            w_o = c_w_out[j]
        xa = _post(o, xa, mods, g, w_o.astype(BF16), ffn_w_gu[i].astype(BF16), ffn_w_down[i].astype(BF16),
                   n_lat_tiles, n_lat_tiles if last else n_all_tiles)
    return xa[None, :n_lat]
```

```python
import functools
import math

import jax
import jax.numpy as jnp
from jax import lax
from jax.experimental import pallas as pl
from jax.experimental.pallas import tpu as pltpu

D_MODEL = 1024
DEPTH = 4
GRID_W = 64
N_MIXERS = 3
BLOCK = 128
ROPE_BASE = 10000.0
RMS_EPS = 1e-6
N_MOD = 6

A_HEADS = 8
A_KV_HEADS = 2
A_GROUP = A_HEADS // A_KV_HEADS
A_HEAD_DIM = D_MODEL // A_HEADS
A_WINDOW = 128
A_SCALE = A_HEAD_DIM ** -0.5
A_Q_W = A_HEADS * A_HEAD_DIM
A_KV_W = A_KV_HEADS * A_HEAD_DIM
A_QKV = A_Q_W + 2 * A_KV_W

B_WIDTH = D_MODEL
B_BLOCKS = 4
B_BLOCK_W = B_WIDTH // B_BLOCKS
B_CONV_W = 4
B_LRU_C = 8.0

C_HEADS = 8
C_NOPE = 128
C_ROPE = 64
C_VDIM = 128
C_Q_RANK = D_MODEL // 2
C_KV_RANK = D_MODEL // 4
C_SCALE = (C_NOPE + C_ROPE) ** -0.5
C_QSCALE = C_SCALE * math.log2(math.e)

FFN_HIDDEN = ((8 * D_MODEL + 3 * 256 - 1) // (3 * 256)) * 256

LANE = 128
SUBLANE = 8
ROW_TILE = 256
FFN_CHUNK = 256
C_SAFE_SUM = 2.0 ** 60
VMEM_LIMIT = 56 * 1024 * 1024
NEG = -1e30

BF16 = jnp.bfloat16
F32 = jnp.float32


def _dot(a, b):
    return jnp.dot(a, b, preferred_element_type=F32)


def _dot_nt(a, b):
    return lax.dot_general(a, b, (((1,), (1,)), ((), ())), preferred_element_type=F32)


def _resident(shape):
    nd = len(shape)
    return pl.BlockSpec(shape, lambda *_: (0,) * nd, pipeline_mode=pl.Buffered(1))


def _params(*sem):
    return pltpu.CompilerParams(dimension_semantics=sem, vmem_limit_bytes=VMEM_LIMIT)


def _rms(x, g):
    ms = jnp.mean(x * x, axis=-1, keepdims=True)
    return x * lax.rsqrt(ms + RMS_EPS) * g


def _norm_mod(x, g, shift, scale):
    return _rms(x, g) * (1.0 + scale) + shift


def _rope(z, cos, sin, half):
    lane = lax.broadcasted_iota(jnp.int32, z.shape, 1)
    first = (lane % (2 * half)) < half
    partner = jnp.where(first, pltpu.roll(z, LANE - half, 1), pltpu.roll(z, half, 1))
    return z * cos + partner * sin


def _ada_kernel(s_ref, w_ref, b_ref, o_ref):
    s = s_ref[...]
    s = s * jax.nn.sigmoid(s)
    o_ref[0] = _dot(s.astype(BF16), w_ref[0].astype(BF16)) + b_ref[0]


def _ada(s_rows, ada_w, ada_b):
    tn = 1536
    depth, d, n = ada_w.shape
    return pl.pallas_call(
        _ada_kernel,
        grid=(depth, n // tn),
        in_specs=[
            pl.BlockSpec((SUBLANE, d), lambda l, j: (0, 0)),
            pl.BlockSpec((1, d, tn), lambda l, j: (l, 0, j)),
            pl.BlockSpec((1, 1, tn), lambda l, j: (l, 0, j)),
        ],
        out_specs=pl.BlockSpec((1, SUBLANE, tn), lambda l, j: (l, 0, j)),
        out_shape=jax.ShapeDtypeStruct((depth, SUBLANE, n), F32),
        compiler_params=_params("parallel", "parallel"),
        name="ada",
    )(s_rows, ada_w, ada_b.reshape(depth, 1, n))


def _mods_spec(n_lat_tiles):
    return pl.BlockSpec((1, N_MOD, D_MODEL), lambda i: (jnp.where(i >= n_lat_tiles, 1, 0), 0, 0))


def _stream(arr, n_lat_tiles):
    return arr, arr, n_lat_tiles


def _stream_specs(width, n_lat_tiles, ctx_tile):
    return [pl.BlockSpec((ROW_TILE, width), lambda i: (jnp.minimum(i, n_lat_tiles - 1), 0)),
            pl.BlockSpec((ROW_TILE, width), lambda i: (ctx_tile, 0))]


def _stream_tile(n_lat_tiles, lat_ref, ctx_ref):
    return jnp.where(pl.program_id(0) >= n_lat_tiles, ctx_ref[...], lat_ref[...])


def _post_kernel(n_lat_tiles, ol_ref, oc_ref, xl_ref, xc_ref, mods_ref, g_ref, wo_ref, wgu_ref, wd_ref, out_ref, act_ref):
    m = mods_ref[0]
    g = g_ref[...]
    y = _dot(_stream_tile(n_lat_tiles, ol_ref, oc_ref), wo_ref[...])
    x1 = _stream_tile(n_lat_tiles, xl_ref, xc_ref) + m[2:3] * _rms(y, g[1:2])
    h = _norm_mod(x1, g[2:3], m[3:4], m[4:5]).astype(BF16)
    for c in range(FFN_HIDDEN // FFN_CHUNK):
        lo = c * FFN_CHUNK
        zg = _dot(h, wgu_ref[:, lo:lo + FFN_CHUNK])
        zu = _dot(h, wgu_ref[:, FFN_HIDDEN + lo:FFN_HIDDEN + lo + FFN_CHUNK])
        act_ref[:, lo:lo + FFN_CHUNK] = (zg * jax.nn.sigmoid(zg) * zu).astype(BF16)
    f = _dot(act_ref[...], wd_ref[...])
    out_ref[...] = x1 + m[5:6] * _rms(f, g[3:4])


def _post(o_stream, x_stream, mods, g, w_o, w_gu, w_down, n_lat_tiles, n_tiles):
    d = D_MODEL
    row = lambda i: (i, 0)
    o_lat, o_ctx, o_ctx_tile = o_stream
    x_lat, x_ctx, x_ctx_tile = x_stream
    return pl.pallas_call(
        functools.partial(_post_kernel, n_lat_tiles),
        grid=(n_tiles,),
        in_specs=_stream_specs(w_o.shape[0], n_lat_tiles, o_ctx_tile) + _stream_specs(d, n_lat_tiles, x_ctx_tile) + [
            _mods_spec(n_lat_tiles),
            _resident(g.shape),
            _resident(w_o.shape),
            _resident(w_gu.shape),
            _resident(w_down.shape),
        ],
        out_specs=pl.BlockSpec((ROW_TILE, d), row),
        out_shape=jax.ShapeDtypeStruct((n_tiles * ROW_TILE, d), F32),
        scratch_shapes=[pltpu.VMEM((ROW_TILE, FFN_HIDDEN), BF16)],
        compiler_params=_params("parallel"),
        name="post_ffn",
    )(o_lat, o_ctx, x_lat, x_ctx, mods, g, w_o, w_gu, w_down)


def _a_proj_kernel(n_lat_tiles, xl_ref, xc_ref, mods_ref, g_ref, w_ref, cos_ref, sin_ref, q_ref, k_ref, v_ref):
    m = mods_ref[0]
    h = _norm_mod(_stream_tile(n_lat_tiles, xl_ref, xc_ref), g_ref[0:1], m[0:1], m[1:2]).astype(BF16)
    z = _dot(h, w_ref[...])
    cos = cos_ref[...]
    sin = sin_ref[...]
    half = A_HEAD_DIM // 4
    for hd in range(A_HEADS):
        lo = hd * A_HEAD_DIM
        q_ref[:, lo:lo + A_HEAD_DIM] = _rope(z[:, lo:lo + A_HEAD_DIM], cos, sin, half).astype(BF16)
    for hd in range(A_KV_HEADS):
        lo = hd * A_HEAD_DIM
        k_ref[:, lo:lo + A_HEAD_DIM] = _rope(z[:, A_Q_W + lo:A_Q_W + lo + A_HEAD_DIM], cos, sin, half).astype(BF16)
    v_ref[...] = z[:, A_Q_W + A_KV_W:].astype(BF16)


def _a_proj(x_stream, mods, g, w_qkv, cos, sin, n_lat_tiles):
    n_tiles = n_lat_tiles + 1
    n_rows = n_tiles * ROW_TILE
    row = lambda i: (i, 0)
    x_lat, x_ctx, x_ctx_tile = x_stream
    return pl.pallas_call(
        functools.partial(_a_proj_kernel, n_lat_tiles),
        grid=(n_tiles,),
        in_specs=_stream_specs(D_MODEL, n_lat_tiles, x_ctx_tile) + [
            _mods_spec(n_lat_tiles),
            _resident(g.shape),
            _resident(w_qkv.shape),
            pl.BlockSpec((ROW_TILE, LANE), row),
            pl.BlockSpec((ROW_TILE, LANE), row),
        ],
        out_specs=[
            pl.BlockSpec((ROW_TILE, A_Q_W), row),
            pl.BlockSpec((ROW_TILE, A_KV_W), row),
            pl.BlockSpec((ROW_TILE, A_KV_W), row),
        ],
        out_shape=[
            jax.ShapeDtypeStruct((n_rows, A_Q_W), BF16),
            jax.ShapeDtypeStruct((n_rows, A_KV_W), BF16),
            jax.ShapeDtypeStruct((n_rows, A_KV_W), BF16),
        ],
        compiler_params=_params("parallel"),
        name="a_proj",
    )(x_lat, x_ctx, mods, g, w_qkv, cos, sin)


def _a_attn_kernel(nb, sink_ref, q_ref, kp_ref, kc_ref, kn_ref, kx_ref, vp_ref, vc_ref, vn_ref, vx_ref, o_ref):
    i = pl.program_id(0)
    is_lat = i < nb
    prev_ok = jnp.logical_and(is_lat, i >= 1)
    next_ok = i < nb - 1
    rows = A_GROUP * BLOCK
    r = lax.broadcasted_iota(jnp.int32, (rows, BLOCK), 0) % BLOCK
    c = lax.broadcasted_iota(jnp.int32, (rows, BLOCK), 1)
    mask_p = jnp.logical_and(c >= r, prev_ok)
    mask_n = jnp.logical_and(c <= r, next_ok)
    grp = lax.broadcasted_iota(jnp.int32, (rows, 1), 0) // BLOCK
    for kh in range(A_KV_HEADS):
        ks = slice(kh * A_HEAD_DIM, (kh + 1) * A_HEAD_DIM)
        qg = jnp.concatenate(
            [q_ref[:, (kh * A_GROUP + gi) * A_HEAD_DIM:(kh * A_GROUP + gi + 1) * A_HEAD_DIM] for gi in range(A_GROUP)],
            axis=0)
        sink = jnp.zeros((rows, 1), F32)
        for gi in range(A_GROUP):
            sink = jnp.where(grp == gi, sink_ref[kh * A_GROUP + gi], sink)
        s_x = _dot_nt(qg, kx_ref[:, ks]) * A_SCALE
        s_p = jnp.where(mask_p, _dot_nt(qg, kp_ref[:, ks]) * A_SCALE, NEG)
        s_c = jnp.where(is_lat, _dot_nt(qg, kc_ref[:, ks]) * A_SCALE, NEG)
        s_n = jnp.where(mask_n, _dot_nt(qg, kn_ref[:, ks]) * A_SCALE, NEG)
        mx = jnp.maximum(jnp.max(s_x, axis=-1, keepdims=True), jnp.max(s_c, axis=-1, keepdims=True))
        mx = jnp.maximum(mx, jnp.maximum(jnp.max(s_p, axis=-1, keepdims=True), jnp.max(s_n, axis=-1, keepdims=True)))
        mx = jnp.maximum(mx, sink)
        e_x = jnp.exp(s_x - mx)
        e_p = jnp.exp(s_p - mx)
        e_c = jnp.exp(s_c - mx)
        e_n = jnp.exp(s_n - mx)
        den = (jnp.exp(sink - mx) + jnp.sum(e_x, axis=-1, keepdims=True) + jnp.sum(e_p, axis=-1, keepdims=True)
               + jnp.sum(e_c, axis=-1, keepdims=True) + jnp.sum(e_n, axis=-1, keepdims=True))
        o = (_dot(e_x.astype(BF16), vx_ref[:, ks]) + _dot(e_p.astype(BF16), vp_ref[:, ks])
             + _dot(e_c.astype(BF16), vc_ref[:, ks]) + _dot(e_n.astype(BF16), vn_ref[:, ks]))
        o = o / den
        for gi in range(A_GROUP):
            hd = kh * A_GROUP + gi
            o_ref[:, hd * A_HEAD_DIM:(hd + 1) * A_HEAD_DIM] = o[gi * BLOCK:(gi + 1) * BLOCK].astype(BF16)


def _a_attn(q, k, v, sinks, n_lat, n_ctx, with_ctx_queries):
    nb = n_lat // BLOCK
    n_blocks = nb + (n_ctx // BLOCK if with_ctx_queries else 0)
    last = (n_lat + n_ctx) // BLOCK - 1
    cur = lambda i: (i, 0)
    prev = lambda i: (jnp.maximum(i - 1, 0), 0)
    nxt = lambda i: (jnp.minimum(i + 1, last), 0)
    ctx = lambda i: (n_lat // n_ctx, 0)
    kv_specs = [
        pl.BlockSpec((BLOCK, A_KV_W), prev),
        pl.BlockSpec((BLOCK, A_KV_W), cur),
        pl.BlockSpec((BLOCK, A_KV_W), nxt),
        pl.BlockSpec((n_ctx, A_KV_W), ctx),
    ]
    return pl.pallas_call(
        functools.partial(_a_attn_kernel, nb),
        grid=(n_blocks,),
        in_specs=[pl.BlockSpec(memory_space=pltpu.SMEM), pl.BlockSpec((BLOCK, A_Q_W), cur)] + kv_specs + kv_specs,
        out_specs=pl.BlockSpec((BLOCK, A_Q_W), cur),
        out_shape=jax.ShapeDtypeStruct((n_blocks * BLOCK, A_Q_W), BF16),
        compiler_params=_params("parallel"),
        name="a_attn",
    )(sinks, q, k, k, k, k, v, v, v, v)


def _b_proj_kernel(x_ref, mods_ref, g_ref, w_ref, xpre_ref, gate_ref):
    m = mods_ref[0]
    h = _norm_mod(x_ref[...], g_ref[0:1], m[0:1], m[1:2]).astype(BF16)
    xpre_ref[...] = _dot(h, w_ref[:, :B_WIDTH])
    gate_ref[...] = jax.nn.gelu(_dot(h, w_ref[:, B_WIDTH:]), approximate=True)


def _b_proj(xa, mods, g, w_in, n_lat_tiles):
    n_tiles = xa.shape[0] // ROW_TILE
    row = lambda i: (i, 0)
    return pl.pallas_call(
        _b_proj_kernel,
        grid=(n_tiles,),
        in_specs=[
            pl.BlockSpec((ROW_TILE, D_MODEL), row),
            _mods_spec(n_lat_tiles),
            _resident(g.shape),
            _resident(w_in.shape),
        ],
        out_specs=[pl.BlockSpec((ROW_TILE, B_WIDTH), row), pl.BlockSpec((ROW_TILE, B_WIDTH), row)],
        out_shape=[jax.ShapeDtypeStruct((xa.shape[0], B_WIDTH), F32)] * 2,
        compiler_params=_params("parallel"),
        name="b_proj",
    )(xa, mods, g, w_in)


def _softplus(y):
    return jnp.maximum(y, 0.0) + jnp.log1p(jnp.exp(-jnp.abs(y)))


def _expm1(x, u):
    one = u == 1.0
    k = (u - 1.0) * x / jnp.where(one, 1.0, jnp.log(u))
    return jnp.where(one, x, jnp.where(x < -1.0, u - 1.0, k))


def _b_tile_index(reverse, n_lat_tiles, i):
    lat = (n_lat_tiles - i) if reverse else (i - 1)
    return jnp.where(i == 0, n_lat_tiles, lat)


def _b_scan_kernel(reverse, n_lat_tiles, *refs):
    if reverse:
        (xc_ref, xp_ref, xn_ref, cw_ref, cb_ref, lam_ref, wa_ref, ba_ref, wx_ref, bx_ref,
         hf_ref, gate_ref, out_ref, carry_ref) = refs
    else:
        (xc_ref, xp_ref, xn_ref, cw_ref, cb_ref, lam_ref, wa_ref, ba_ref, wx_ref, bx_ref,
         out_ref, carry_ref) = refs
    i = pl.program_id(0)
    tile = _b_tile_index(reverse, n_lat_tiles, i)

    @pl.when(i == 0)
    def _():
        carry_ref[...] = jnp.zeros(carry_ref.shape, F32)

    prev_ok = jnp.logical_and(tile >= 1, tile <= n_lat_tiles - 1)
    next_ok = tile <= n_lat_tiles - 2
    u = xc_ref[...]
    n = u.shape[0]
    row = lax.broadcasted_iota(jnp.int32, (n, 1), 0)
    pm = jnp.where(prev_ok, xp_ref[SUBLANE - 1:SUBLANE, :], 0.0)
    n0 = jnp.where(next_ok, xn_ref[0:1, :], 0.0)
    n1 = jnp.where(next_ok, xn_ref[1:2, :], 0.0)
    um1 = jnp.where(row == 0, pm, pltpu.roll(u, 1, 0))
    up1 = jnp.where(row == n - 1, n0, pltpu.roll(u, n - 1, 0))
    up2 = jnp.where(row == n - 2, n0, jnp.where(row == n - 1, n1, pltpu.roll(u, n - 2, 0)))
    cw = cw_ref[...]
    xb = cb_ref[...] + (cw[0:1] * um1 + cw[1:2] * u + cw[2:3] * up1 + cw[3:4] * up2)

    xb16 = xb.astype(BF16)
    rs, gs = [], []
    for nblk in range(B_BLOCKS):
        sl = slice(nblk * B_BLOCK_W, (nblk + 1) * B_BLOCK_W)
        rs.append(_dot(xb16[:, sl], wa_ref[nblk]))
        gs.append(_dot(xb16[:, sl], wx_ref[nblk]))
    r = jax.nn.sigmoid(jnp.concatenate(rs, axis=1) + ba_ref[...])
    gi = jax.nn.sigmoid(jnp.concatenate(gs, axis=1) + bx_ref[...])
    log_a = (-B_LRU_C * r) * _softplus(-lam_ref[...])
    a = jnp.exp(log_a)
    b = jnp.sqrt(-_expm1(2.0 * log_a, a * a)) * (gi * xb)

    sub = row % SUBLANE
    s = 1
    while s < SUBLANE:
        if reverse:
            shift, valid = n - s, sub < SUBLANE - s
        else:
            shift, valid = s, sub >= s
        a_sh = pltpu.roll(a, shift, 0)
        b_sh = pltpu.roll(b, shift, 0)
        b = jnp.where(valid, a * b_sh + b, b)
        a = jnp.where(valid, a * a_sh, a)
        s *= 2
    carry = carry_ref[0:1, :]
    n_groups = n // SUBLANE
    hs = [None] * n_groups
    for k in range(n_groups):
        gidx = n_groups - 1 - k if reverse else k
        rows = slice(gidx * SUBLANE, (gidx + 1) * SUBLANE)
        hg = a[rows] * carry + b[rows]
        carry = hg[0:1, :] if reverse else hg[SUBLANE - 1:SUBLANE, :]
        hs[gidx] = hg
    carry_ref[0:1, :] = carry
    h = jnp.concatenate(hs, axis=0)
    if reverse:
        out_ref[...] = ((hf_ref[...] + h) * gate_ref[...]).astype(out_ref.dtype)
    else:
        out_ref[...] = h


def _b_scan(reverse, xpre, conv_w, conv_b, lam, w_a, b_a, w_x, b_x, n_lat_tiles, hf=None, gate=None):
    n_tiles = xpre.shape[0] // ROW_TILE
    per8 = ROW_TILE // SUBLANE
    last8 = xpre.shape[0] // SUBLANE - 1
    tile = functools.partial(_b_tile_index, reverse, n_lat_tiles)
    cur = lambda i: (tile(i), 0)
    prev8 = lambda i: (jnp.maximum(tile(i) * per8 - 1, 0), 0)
    next8 = lambda i: (jnp.minimum((tile(i) + 1) * per8, last8), 0)
    w = B_WIDTH
    in_specs = [
        pl.BlockSpec((ROW_TILE, w), cur),
        pl.BlockSpec((SUBLANE, w), prev8),
        pl.BlockSpec((SUBLANE, w), next8),
        _resident(conv_w.shape), _resident(conv_b.shape), _resident(lam.shape),
        _resident(w_a.shape), _resident(b_a.shape), _resident(w_x.shape), _resident(b_x.shape),
    ]
    args = [xpre, xpre, xpre, conv_w, conv_b, lam, w_a, b_a, w_x, b_x]
    if reverse:
        in_specs += [pl.BlockSpec((ROW_TILE, w), cur), pl.BlockSpec((ROW_TILE, w), cur)]
        args += [hf, gate]
    return pl.pallas_call(
        functools.partial(_b_scan_kernel, reverse, n_lat_tiles),
        grid=(n_tiles,),
        in_specs=in_specs,
        out_specs=pl.BlockSpec((ROW_TILE, w), cur),
        out_shape=jax.ShapeDtypeStruct(xpre.shape, BF16 if reverse else F32),
        scratch_shapes=[pltpu.VMEM((SUBLANE, w), F32)],
        compiler_params=_params("arbitrary"),
        name="b_scan_rev" if reverse else "b_scan_fwd",
    )(*args)


C_IN_PAD = C_Q_RANK + C_KV_RANK + LANE
C_QR_PAD = C_HEADS * LANE


def _c_proj_kernel(x_ref, mods_ref, g_ref, win_ref, gq_ref, gkv_ref, wuq_ref, wuk_ref, wuvt_ref, cos_ref, sin_ref,
                   qn_ref, qr_ref, kn_ref, kr_ref, vt_ref):
    m = mods_ref[0]
    h = _norm_mod(x_ref[...], g_ref[0:1], m[0:1], m[1:2]).astype(BF16)
    z = _dot(h, win_ref[...])
    cq = _rms(z[:, :C_Q_RANK], gq_ref[...]).astype(BF16)
    ckv = _rms(z[:, C_Q_RANK:C_Q_RANK + C_KV_RANK], gkv_ref[...]).astype(BF16)
    cos = cos_ref[...]
    sin = sin_ref[...]
    half = C_ROPE // 4
    kr_ref[...] = _rope(z[:, C_Q_RANK + C_KV_RANK:], cos, sin, half).astype(BF16)
    q = _dot(cq, wuq_ref[...]) * C_QSCALE
    qn_ref[...] = q[:, :C_HEADS * C_NOPE].astype(BF16)
    for hd in range(C_HEADS):
        lo = C_HEADS * C_NOPE + hd * LANE
        qr_ref[:, hd * LANE:(hd + 1) * LANE] = _rope(q[:, lo:lo + LANE], cos, sin, half).astype(BF16)
    kn_ref[...] = _dot(ckv, wuk_ref[...]).astype(BF16)
    vt_ref[...] = _dot_nt(wuvt_ref[...], ckv).astype(BF16)


def _c_proj(xa, mods, g, w_in, g_q, g_kv, w_uq, w_uk, w_uvt, cos, sin, n_lat_tiles):
    n_rows = xa.shape[0]
    n_tiles = n_rows // ROW_TILE
    row = lambda i: (i, 0)
    hw = C_HEADS * C_NOPE
    return pl.pallas_call(
        _c_proj_kernel,
        grid=(n_tiles,),
        in_specs=[
            pl.BlockSpec((ROW_TILE, D_MODEL), row),
            _mods_spec(n_lat_tiles),
            _resident(g.shape), _resident(w_in.shape), _resident(g_q.shape), _resident(g_kv.shape),
            _resident(w_uq.shape), _resident(w_uk.shape), _resident(w_uvt.shape),
            pl.BlockSpec((ROW_TILE, LANE), row),
            pl.BlockSpec((ROW_TILE, LANE), row),
        ],
        out_specs=[
            pl.BlockSpec((ROW_TILE, hw), row),
            pl.BlockSpec((ROW_TILE, C_QR_PAD), row),
            pl.BlockSpec((ROW_TILE, hw), row),
            pl.BlockSpec((ROW_TILE, LANE), row),
            pl.BlockSpec((C_HEADS * C_VDIM, ROW_TILE), lambda i: (0, i)),
        ],
        out_shape=[
            jax.ShapeDtypeStruct((n_rows, hw), BF16),
            jax.ShapeDtypeStruct((n_rows, C_QR_PAD), BF16),
            jax.ShapeDtypeStruct((n_rows, hw), BF16),
            jax.ShapeDtypeStruct((n_rows, LANE), BF16),
            jax.ShapeDtypeStruct((C_HEADS * C_VDIM, n_rows), BF16),
        ],
        compiler_params=_params("parallel"),
        name="c_proj",
    )(xa, mods, g, w_in, g_q, g_kv, w_uq, w_uk, w_uvt, cos, sin)


def _c_scores(hd, qn_ref, qr_ref, kn_ref, kr_ref):
    hs = slice(hd * LANE, (hd + 1) * LANE)
    qcat = jnp.concatenate([qn_ref[:, hs], qr_ref[:, hs]], axis=1)
    kcat = jnp.concatenate([kn_ref[:, hs], kr_ref[...]], axis=1)
    return _dot_nt(kcat, qcat)


def _c_key_block_exact(qn_ref, qr_ref, kn_ref, kr_ref, vt_ref, m_ref, l_ref, acc_ref):
    for hd in range(C_HEADS):
        hs = slice(hd * LANE, (hd + 1) * LANE)
        st = _c_scores(hd, qn_ref, qr_ref, kn_ref, kr_ref)
        m_old = m_ref[hd:hd + 1, :]
        m_new = jnp.maximum(m_old, jnp.max(st, axis=0, keepdims=True))
        alpha = jnp.exp2(m_old - m_new)
        p = jnp.exp2(st - m_new)
        l_ref[hd:hd + 1, :] = alpha * l_ref[hd:hd + 1, :] + jnp.sum(p, axis=0, keepdims=True)
        acc_ref[hd] = alpha * acc_ref[hd] + _dot(vt_ref[hs, :], p.astype(BF16))
        m_ref[hd:hd + 1, :] = m_new


def _c_key_block_fixed_ref(qn_ref, qr_ref, kn_ref, kr_ref, vt_ref, m_ref, ps_ref, pv_ref):
    for hd in range(C_HEADS):
        hs = slice(hd * LANE, (hd + 1) * LANE)
        p = jnp.exp2(_c_scores(hd, qn_ref, qr_ref, kn_ref, kr_ref) - m_ref[hd:hd + 1, :])
        ps_ref[hd:hd + 1, :] = jnp.sum(p, axis=0, keepdims=True)
        pv_ref[hd] = _dot(vt_ref[hs, :], p.astype(BF16))


def _c_init(m_ref, l_ref, acc_ref):
    m_ref[...] = jnp.full(m_ref.shape, NEG, F32)
    l_ref[...] = jnp.zeros(l_ref.shape, F32)
    acc_ref[...] = jnp.zeros(acc_ref.shape, F32)


def _c_finish(l_ref, acc_ref, o_ref):
    for hd in range(C_HEADS):
        ot = acc_ref[hd] / l_ref[hd:hd + 1, :]
        o_ref[:, hd * C_VDIM:(hd + 1) * C_VDIM] = ot.T.astype(o_ref.dtype)


def _c_attn_lat_kernel(qn_ref, qr_ref, kn_ref, kr_ref, vt_ref, knx_ref, krx_ref, vtx_ref, o_ref,
                       m_ref, l_ref, acc_ref, ps_ref, pv_ref):
    j = pl.program_id(1)

    @pl.when(j == 0)
    def _():
        _c_init(m_ref, l_ref, acc_ref)
        _c_key_block_exact(qn_ref, qr_ref, knx_ref, krx_ref, vtx_ref, m_ref, l_ref, acc_ref)

    _c_key_block_fixed_ref(qn_ref, qr_ref, kn_ref, kr_ref, vt_ref, m_ref, ps_ref, pv_ref)
    safe = jnp.max(ps_ref[...]) < C_SAFE_SUM

    @pl.when(safe)
    def _():
        l_ref[...] = l_ref[...] + ps_ref[...]
        acc_ref[...] = acc_ref[...] + pv_ref[...]

    @pl.when(jnp.logical_not(safe))
    def _():
        _c_key_block_exact(qn_ref, qr_ref, kn_ref, kr_ref, vt_ref, m_ref, l_ref, acc_ref)

    @pl.when(j == pl.num_programs(1) - 1)
    def _():
        _c_finish(l_ref, acc_ref, o_ref)


def _c_attn_ctx_kernel(qn_ref, qr_ref, knx_ref, krx_ref, vtx_ref, o_ref, m_ref, l_ref, acc_ref):
    _c_init(m_ref, l_ref, acc_ref)
    _c_key_block_exact(qn_ref, qr_ref, knx_ref, krx_ref, vtx_ref, m_ref, l_ref, acc_ref)
    _c_finish(l_ref, acc_ref, o_ref)


def _c_scratch(tq, with_uncommitted):
    stats = [pltpu.VMEM((C_HEADS, tq), F32), pltpu.VMEM((C_HEADS, tq), F32), pltpu.VMEM((C_HEADS, C_VDIM, tq), F32)]
    return stats + (stats[1:] if with_uncommitted else [])


def _c_attn(qn, qr, kn, kr, vt, n_lat, n_ctx, with_ctx_queries):
    tq = min(512, n_lat)
    tk = min(2048, n_lat)
    hw = C_HEADS * C_NOPE
    cb = n_lat // n_ctx
    qrow = lambda i, j: (i, 0)
    krow = lambda i, j: (j, 0)
    o = pl.pallas_call(
        _c_attn_lat_kernel,
        grid=(n_lat // tq, n_lat // tk),
        in_specs=[
            pl.BlockSpec((tq, hw), qrow),
            pl.BlockSpec((tq, C_QR_PAD), qrow),
            pl.BlockSpec((tk, hw), krow),
            pl.BlockSpec((tk, LANE), krow),
            pl.BlockSpec((hw, tk), lambda i, j: (0, j)),
            pl.BlockSpec((n_ctx, hw), lambda i, j: (cb, 0)),
            pl.BlockSpec((n_ctx, LANE), lambda i, j: (cb, 0)),
            pl.BlockSpec((hw, n_ctx), lambda i, j: (0, cb)),
        ],
        out_specs=pl.BlockSpec((tq, hw), qrow),
        out_shape=jax.ShapeDtypeStruct((n_lat, hw), BF16),
        scratch_shapes=_c_scratch(tq, True),
        compiler_params=_params("parallel", "arbitrary"),
        name="c_attn_lat",
    )(qn, qr, kn, kr, vt, kn, kr, vt)
    if not with_ctx_queries:
        return o, o, 0
    o_ctx = pl.pallas_call(
        _c_attn_ctx_kernel,
        grid=(1,),
        in_specs=[
            pl.BlockSpec((n_ctx, hw), lambda i: (cb, 0)),
            pl.BlockSpec((n_ctx, C_QR_PAD), lambda i: (cb, 0)),
            pl.BlockSpec((n_ctx, hw), lambda i: (cb, 0)),
            pl.BlockSpec((n_ctx, LANE), lambda i: (cb, 0)),
            pl.BlockSpec((hw, n_ctx), lambda i: (0, cb)),
        ],
        out_specs=pl.BlockSpec((n_ctx, hw), lambda i: (0, 0)),
        out_shape=jax.ShapeDtypeStruct((n_ctx, hw), BF16),
        scratch_shapes=_c_scratch(n_ctx, False),
        compiler_params=_params("arbitrary"),
        name="c_attn_ctx",
    )(qn, qr, kn, kr, vt)
    return o, o_ctx, 0


def _rope_tables(n_lat, n_ctx, d_rot):
    n_rows = n_lat // GRID_W
    row = jnp.repeat(jnp.arange(n_rows), GRID_W).astype(F32)
    col = jnp.tile(jnp.arange(GRID_W), n_rows).astype(F32)
    d_axis = d_rot // 2
    inv = 1.0 / (ROPE_BASE ** (jnp.arange(0, d_axis, 2, dtype=F32) / d_axis))
    ar = row[:, None] * inv
    ac = col[:, None] * inv
    cos = jnp.concatenate([jnp.cos(ar), jnp.cos(ar), jnp.cos(ac), jnp.cos(ac)], axis=1)
    sin = jnp.concatenate([-jnp.sin(ar), jnp.sin(ar), -jnp.sin(ac), jnp.sin(ac)], axis=1)
    cos = jnp.pad(cos, ((0, 0), (0, LANE - d_rot)))
    sin = jnp.pad(sin, ((0, 0), (0, LANE - d_rot)))
    cos_ctx = jnp.pad(jnp.ones((n_ctx, d_rot), F32), ((0, 0), (0, LANE - d_rot)))
    return (jnp.concatenate([cos, cos_ctx], axis=0),
            jnp.concatenate([sin, jnp.zeros((n_ctx, LANE), F32)], axis=0))


def _pad_cols(w, width):
    return jnp.pad(w, ((0, 0), (0, width - w.shape[1])))


def kernel(x, c, ctx, c_ctx, ada_w, ada_b, norm_g, ffn_w_gu, ffn_w_down, a_w_qkv, a_sinks, a_w_o, b_w_in, b_conv_w, b_conv_b, b_lam, b_w_a, b_b_a, b_w_x, b_b_x, b_w_out, c_w_in, c_g_q, c_g_kv, c_w_uq, c_w_ukv, c_w_out):
    assert x.shape[0] == 1 and ctx.shape[0] == 1 and x.shape[2] == D_MODEL
    n_lat, n_ctx = x.shape[1], ctx.shape[1]
    assert n_ctx == ROW_TILE and n_lat % ROW_TILE == 0 and n_lat % GRID_W == 0
    n_lat_tiles = n_lat // ROW_TILE
    n_all_tiles = n_lat_tiles + 1

    x_stream = (x[0], ctx[0], 0)
    s_rows = jnp.zeros((SUBLANE, D_MODEL), F32).at[0].set(c[0]).at[1].set(c_ctx)
    mods_all = _ada(s_rows, ada_w, ada_b)[:, :2].reshape(DEPTH, 2, N_MOD, D_MODEL)
    cos_a, sin_a = _rope_tables(n_lat, n_ctx, A_HEAD_DIM)
    cos_c, sin_c = _rope_tables(n_lat, n_ctx, C_ROPE)

    for i in range(DEPTH):
        last = i == DEPTH - 1
        kind, j = i % N_MIXERS, i // N_MIXERS
        mods = mods_all[i]
        g = norm_g[i]
        if kind == 0:
            q, k, v = _a_proj(x_stream, mods, g, a_w_qkv[j].astype(BF16), cos_a, sin_a, n_lat_tiles)
            o = _a_attn(q, k, v, a_sinks[j], n_lat, n_ctx, not last)
            o_stream = _stream(o, n_lat_tiles) if not last else (o, o, 0)
            w_o = a_w_o[j]
        elif kind == 1:
            xa = x_stream[0]
            xpre, gate = _b_proj(xa, mods, g, b_w_in[j].astype(BF16), n_lat_tiles)
            scan_args = lambda d: (b_conv_w[j], b_conv_b[j][None], b_lam[j, d][None], b_w_a[j, d].astype(BF16),
                                   b_b_a[j, d][None], b_w_x[j, d].astype(BF16), b_b_x[j, d][None], n_lat_tiles)
            hf = _b_scan(False, xpre, *scan_args(0))
            o_stream = _stream(_b_scan(True, xpre, *scan_args(1), hf=hf, gate=gate), n_lat_tiles)
            w_o = b_w_out[j]
        else:
            xa = x_stream[0]
            w_in = c_w_in[j]
            w_in_p = _pad_cols(w_in, C_IN_PAD).astype(BF16)
            w_uq3 = c_w_uq[j].reshape(C_Q_RANK, C_HEADS, C_NOPE + C_ROPE)
            w_uq_p = jnp.concatenate(
                [w_uq3[..., :C_NOPE].reshape(C_Q_RANK, C_HEADS * C_NOPE),
                 jnp.pad(w_uq3[..., C_NOPE:], ((0, 0), (0, 0), (0, LANE - C_ROPE))).reshape(C_Q_RANK, C_QR_PAD)],
                axis=1).astype(BF16)
            w_ukv3 = c_w_ukv[j].reshape(C_KV_RANK, C_HEADS, C_NOPE + C_VDIM)
            w_uk = w_ukv3[..., :C_NOPE].reshape(C_KV_RANK, C_HEADS * C_NOPE).astype(BF16)
            w_uvt = w_ukv3[..., C_NOPE:].reshape(C_KV_RANK, C_HEADS * C_VDIM).T.astype(BF16)
            qn, qr, kn, kr, vt = _c_proj(xa, mods, g, w_in_p, c_g_q[j][None], c_g_kv[j][None], w_uq_p, w_uk, w_uvt,
                                         cos_c, sin_c, n_lat_tiles)
            o_stream = _c_attn(qn, qr, kn, kr, vt, n_lat, n_ctx, not last)
            w_o = c_w_out[j]
        xa = _post(o_stream, x_stream, mods, g, w_o.astype(BF16), ffn_w_gu[i], ffn_w_down[i],
                   n_lat_tiles, n_lat_tiles if last else n_all_tiles)
        x_stream = _stream(xa, n_lat_tiles)
    return xa[None, :n_lat]
```

```python
import functools
import math

import jax
import jax.numpy as jnp
from jax import lax
from jax.experimental import pallas as pl
from jax.experimental.pallas import tpu as pltpu

D_MODEL = 1024
DEPTH = 4
GRID_W = 64
N_MIXERS = 3
BLOCK = 128
ROPE_BASE = 10000.0
RMS_EPS = 1e-6
N_MOD = 6

A_HEADS = 8
A_KV_HEADS = 2
A_GROUP = A_HEADS // A_KV_HEADS
A_HEAD_DIM = D_MODEL // A_HEADS
A_WINDOW = 128
A_SCALE = A_HEAD_DIM ** -0.5
A_Q_W = A_HEADS * A_HEAD_DIM
A_KV_W = A_KV_HEADS * A_HEAD_DIM
A_QKV = A_Q_W + 2 * A_KV_W

B_WIDTH = D_MODEL
B_BLOCKS = 4
B_BLOCK_W = B_WIDTH // B_BLOCKS
B_CONV_W = 4
B_LRU_C = 8.0

C_HEADS = 8
C_NOPE = 128
C_ROPE = 64
C_VDIM = 128
C_Q_RANK = D_MODEL // 2
C_KV_RANK = D_MODEL // 4
C_SCALE = (C_NOPE + C_ROPE) ** -0.5
C_QSCALE = C_SCALE * math.log2(math.e)

FFN_HIDDEN = ((8 * D_MODEL + 3 * 256 - 1) // (3 * 256)) * 256

LANE = 128
SUBLANE = 8
ROW_TILE = 256
FFN_CHUNK = 256
C_SAFE_SUM = 2.0 ** 60
VMEM_LIMIT = 56 * 1024 * 1024
NEG = -1e30

BF16 = jnp.bfloat16
F32 = jnp.float32


def _dot(a, b):
    return jnp.dot(a, b, preferred_element_type=F32)


def _dot_nt(a, b):
    return lax.dot_general(a, b, (((1,), (1,)), ((), ())), preferred_element_type=F32)


def _resident(shape):
    nd = len(shape)
    return pl.BlockSpec(shape, lambda *_: (0,) * nd, pipeline_mode=pl.Buffered(1))


def _params(*sem):
    return pltpu.CompilerParams(dimension_semantics=sem, vmem_limit_bytes=VMEM_LIMIT)


def _rms(x, g):
    ms = jnp.mean(x * x, axis=-1, keepdims=True)
    return x * lax.rsqrt(ms + RMS_EPS) * g


def _norm_mod(x, g, shift, scale):
    return _rms(x, g) * (1.0 + scale) + shift


def _rope(z, cos, sin, half):
    lane = lax.broadcasted_iota(jnp.int32, z.shape, 1)
    first = (lane % (2 * half)) < half
    partner = jnp.where(first, pltpu.roll(z, LANE - half, 1), pltpu.roll(z, half, 1))
    return z * cos + partner * sin


def _ada_kernel(s_ref, w_ref, b_ref, o_ref):
    s = s_ref[...]
    s = s * jax.nn.sigmoid(s)
    o_ref[0] = _dot(s.astype(BF16), w_ref[0].astype(BF16)) + b_ref[0]


def _ada(s_rows, ada_w, ada_b):
    tn = 1536
    depth, d, n = ada_w.shape
    return pl.pallas_call(
        _ada_kernel,
        grid=(depth, n // tn),
        in_specs=[
            pl.BlockSpec((SUBLANE, d), lambda l, j: (0, 0)),
            pl.BlockSpec((1, d, tn), lambda l, j: (l, 0, j)),
            pl.BlockSpec((1, 1, tn), lambda l, j: (l, 0, j)),
        ],
        out_specs=pl.BlockSpec((1, SUBLANE, tn), lambda l, j: (l, 0, j)),
        out_shape=jax.ShapeDtypeStruct((depth, SUBLANE, n), F32),
        compiler_params=_params("parallel", "parallel"),
        name="ada",
    )(s_rows, ada_w, ada_b.reshape(depth, 1, n))


def _mods_spec(n_lat_tiles):
    return pl.BlockSpec((1, N_MOD, D_MODEL), lambda i: (jnp.where(i >= n_lat_tiles, 1, 0), 0, 0))


def _stream(arr, n_lat_tiles):
    return arr, arr, n_lat_tiles


def _stream_specs(width, n_lat_tiles, ctx_tile):
    return [pl.BlockSpec((ROW_TILE, width), lambda i: (jnp.minimum(i, n_lat_tiles - 1), 0)),
            pl.BlockSpec((ROW_TILE, width), lambda i: (ctx_tile, 0))]


def _stream_tile(n_lat_tiles, lat_ref, ctx_ref):
    return jnp.where(pl.program_id(0) >= n_lat_tiles, ctx_ref[...], lat_ref[...])


def _post_kernel(n_lat_tiles, ol_ref, oc_ref, xl_ref, xc_ref, mods_ref, g_ref, wo_ref, wgu_ref, wd_ref, out_ref, act_ref):
    m = mods_ref[0]
    g = g_ref[...]
    y = _dot(_stream_tile(n_lat_tiles, ol_ref, oc_ref), wo_ref[...])
    x1 = _stream_tile(n_lat_tiles, xl_ref, xc_ref) + m[2:3] * _rms(y, g[1:2])
    h = _norm_mod(x1, g[2:3], m[3:4], m[4:5]).astype(BF16)
    for c in range(FFN_HIDDEN // FFN_CHUNK):
        lo = c * FFN_CHUNK
        zg = _dot(h, wgu_ref[:, lo:lo + FFN_CHUNK])
        zu = _dot(h, wgu_ref[:, FFN_HIDDEN + lo:FFN_HIDDEN + lo + FFN_CHUNK])
        act_ref[:, lo:lo + FFN_CHUNK] = (zg * jax.nn.sigmoid(zg) * zu).astype(BF16)
    f = _dot(act_ref[...], wd_ref[...])
    out_ref[...] = x1 + m[5:6] * _rms(f, g[3:4])


def _post(o_stream, x_stream, mods, g, w_o, w_gu, w_down, n_lat_tiles, n_tiles):
    d = D_MODEL
    row = lambda i: (i, 0)
    o_lat, o_ctx, o_ctx_tile = o_stream
    x_lat, x_ctx, x_ctx_tile = x_stream
    return pl.pallas_call(
        functools.partial(_post_kernel, n_lat_tiles),
        grid=(n_tiles,),
        in_specs=_stream_specs(w_o.shape[0], n_lat_tiles, o_ctx_tile) + _stream_specs(d, n_lat_tiles, x_ctx_tile) + [
            _mods_spec(n_lat_tiles),
            _resident(g.shape),
            _resident(w_o.shape),
            _resident(w_gu.shape),
            _resident(w_down.shape),
        ],
        out_specs=pl.BlockSpec((ROW_TILE, d), row),
        out_shape=jax.ShapeDtypeStruct((n_tiles * ROW_TILE, d), F32),
        scratch_shapes=[pltpu.VMEM((ROW_TILE, FFN_HIDDEN), BF16)],
        compiler_params=_params("parallel"),
        name="post_ffn",
    )(o_lat, o_ctx, x_lat, x_ctx, mods, g, w_o, w_gu, w_down)


def _a_proj_kernel(n_lat_tiles, xl_ref, xc_ref, mods_ref, g_ref, w_ref, cos_ref, sin_ref, q_ref, k_ref, v_ref):
    m = mods_ref[0]
    h = _norm_mod(_stream_tile(n_lat_tiles, xl_ref, xc_ref), g_ref[0:1], m[0:1], m[1:2]).astype(BF16)
    z = _dot(h, w_ref[...])
    cos = cos_ref[...]
    sin = sin_ref[...]
    half = A_HEAD_DIM // 4
    for hd in range(A_HEADS):
        lo = hd * A_HEAD_DIM
        q_ref[:, lo:lo + A_HEAD_DIM] = _rope(z[:, lo:lo + A_HEAD_DIM], cos, sin, half).astype(BF16)
    for hd in range(A_KV_HEADS):
        lo = hd * A_HEAD_DIM
        k_ref[:, lo:lo + A_HEAD_DIM] = _rope(z[:, A_Q_W + lo:A_Q_W + lo + A_HEAD_DIM], cos, sin, half).astype(BF16)
    v_ref[...] = z[:, A_Q_W + A_KV_W:].astype(BF16)


def _a_proj(x_stream, mods, g, w_qkv, cos, sin, n_lat_tiles):
    n_tiles = n_lat_tiles + 1
    n_rows = n_tiles * ROW_TILE
    row = lambda i: (i, 0)
    x_lat, x_ctx, x_ctx_tile = x_stream
    return pl.pallas_call(
        functools.partial(_a_proj_kernel, n_lat_tiles),
        grid=(n_tiles,),
        in_specs=_stream_specs(D_MODEL, n_lat_tiles, x_ctx_tile) + [
            _mods_spec(n_lat_tiles),
            _resident(g.shape),
            _resident(w_qkv.shape),
            pl.BlockSpec((ROW_TILE, LANE), row),
            pl.BlockSpec((ROW_TILE, LANE), row),
        ],
        out_specs=[
            pl.BlockSpec((ROW_TILE, A_Q_W), row),
            pl.BlockSpec((ROW_TILE, A_KV_W), row),
            pl.BlockSpec((ROW_TILE, A_KV_W), row),
        ],
        out_shape=[
            jax.ShapeDtypeStruct((n_rows, A_Q_W), BF16),
            jax.ShapeDtypeStruct((n_rows, A_KV_W), BF16),
            jax.ShapeDtypeStruct((n_rows, A_KV_W), BF16),
        ],
        compiler_params=_params("parallel"),
        name="a_proj",
    )(x_lat, x_ctx, mods, g, w_qkv, cos, sin)


def _a_attn_kernel(nb, sink_ref, q_ref, kp_ref, kc_ref, kn_ref, kx_ref, vp_ref, vc_ref, vn_ref, vx_ref, o_ref):
    i = pl.program_id(0)
    is_lat = i < nb
    prev_ok = jnp.logical_and(is_lat, i >= 1)
    next_ok = i < nb - 1
    rows = A_GROUP * BLOCK
    r = lax.broadcasted_iota(jnp.int32, (rows, BLOCK), 0) % BLOCK
    c = lax.broadcasted_iota(jnp.int32, (rows, BLOCK), 1)
    mask_p = jnp.logical_and(c >= r, prev_ok)
    mask_n = jnp.logical_and(c <= r, next_ok)
    grp = lax.broadcasted_iota(jnp.int32, (rows, 1), 0) // BLOCK
    for kh in range(A_KV_HEADS):
        ks = slice(kh * A_HEAD_DIM, (kh + 1) * A_HEAD_DIM)
        qg = jnp.concatenate(
            [q_ref[:, (kh * A_GROUP + gi) * A_HEAD_DIM:(kh * A_GROUP + gi + 1) * A_HEAD_DIM] for gi in range(A_GROUP)],
            axis=0)
        sink = jnp.zeros((rows, 1), F32)
        for gi in range(A_GROUP):
            sink = jnp.where(grp == gi, sink_ref[kh * A_GROUP + gi], sink)
        s_x = _dot_nt(qg, kx_ref[:, ks]) * A_SCALE
        s_p = jnp.where(mask_p, _dot_nt(qg, kp_ref[:, ks]) * A_SCALE, NEG)
        s_c = jnp.where(is_lat, _dot_nt(qg, kc_ref[:, ks]) * A_SCALE, NEG)
        s_n = jnp.where(mask_n, _dot_nt(qg, kn_ref[:, ks]) * A_SCALE, NEG)
        mx = jnp.maximum(jnp.max(s_x, axis=-1, keepdims=True), jnp.max(s_c, axis=-1, keepdims=True))
        mx = jnp.maximum(mx, jnp.maximum(jnp.max(s_p, axis=-1, keepdims=True), jnp.max(s_n, axis=-1, keepdims=True)))
        mx = jnp.maximum(mx, sink)
        e_x = jnp.exp(s_x - mx)
        e_p = jnp.exp(s_p - mx)
        e_c = jnp.exp(s_c - mx)
        e_n = jnp.exp(s_n - mx)
        den = (jnp.exp(sink - mx) + jnp.sum(e_x, axis=-1, keepdims=True) + jnp.sum(e_p, axis=-1, keepdims=True)
               + jnp.sum(e_c, axis=-1, keepdims=True) + jnp.sum(e_n, axis=-1, keepdims=True))
        o = (_dot(e_x.astype(BF16), vx_ref[:, ks]) + _dot(e_p.astype(BF16), vp_ref[:, ks])
             + _dot(e_c.astype(BF16), vc_ref[:, ks]) + _dot(e_n.astype(BF16), vn_ref[:, ks]))
        o = o / den
        for gi in range(A_GROUP):
            hd = kh * A_GROUP + gi
            o_ref[:, hd * A_HEAD_DIM:(hd + 1) * A_HEAD_DIM] = o[gi * BLOCK:(gi + 1) * BLOCK].astype(BF16)


def _a_attn(q, k, v, sinks, n_lat, n_ctx, with_ctx_queries):
    nb = n_lat // BLOCK
    n_blocks = nb + (n_ctx // BLOCK if with_ctx_queries else 0)
    last = (n_lat + n_ctx) // BLOCK - 1
    cur = lambda i: (i, 0)
    prev = lambda i: (jnp.maximum(i - 1, 0), 0)
    nxt = lambda i: (jnp.minimum(i + 1, last), 0)
    ctx = lambda i: (n_lat // n_ctx, 0)
    kv_specs = [
        pl.BlockSpec((BLOCK, A_KV_W), prev),
        pl.BlockSpec((BLOCK, A_KV_W), cur),
        pl.BlockSpec((BLOCK, A_KV_W), nxt),
        pl.BlockSpec((n_ctx, A_KV_W), ctx),
    ]
    return pl.pallas_call(
        functools.partial(_a_attn_kernel, nb),
        grid=(n_blocks,),
        in_specs=[pl.BlockSpec(memory_space=pltpu.SMEM), pl.BlockSpec((BLOCK, A_Q_W), cur)] + kv_specs + kv_specs,
        out_specs=pl.BlockSpec((BLOCK, A_Q_W), cur),
        out_shape=jax.ShapeDtypeStruct((n_blocks * BLOCK, A_Q_W), BF16),
        compiler_params=_params("parallel"),
        name="a_attn",
    )(sinks, q, k, k, k, k, v, v, v, v)


def _b_proj_kernel(x_ref, mods_ref, g_ref, w_ref, xpre_ref, gate_ref):
    m = mods_ref[0]
    h = _norm_mod(x_ref[...], g_ref[0:1], m[0:1], m[1:2]).astype(BF16)
    xpre_ref[...] = _dot(h, w_ref[:, :B_WIDTH])
    gate_ref[...] = jax.nn.gelu(_dot(h, w_ref[:, B_WIDTH:]), approximate=True)


def _b_proj(xa, mods, g, w_in, n_lat_tiles):
    n_tiles = xa.shape[0] // ROW_TILE
    row = lambda i: (i, 0)
    return pl.pallas_call(
        _b_proj_kernel,
        grid=(n_tiles,),
        in_specs=[
            pl.BlockSpec((ROW_TILE, D_MODEL), row),
            _mods_spec(n_lat_tiles),
            _resident(g.shape),
            _resident(w_in.shape),
        ],
        out_specs=[pl.BlockSpec((ROW_TILE, B_WIDTH), row), pl.BlockSpec((ROW_TILE, B_WIDTH), row)],
        out_shape=[jax.ShapeDtypeStruct((xa.shape[0], B_WIDTH), F32)] * 2,
        compiler_params=_params("parallel"),
        name="b_proj",
    )(xa, mods, g, w_in)


def _softplus(y):
    return jnp.maximum(y, 0.0) + jnp.log1p(jnp.exp(-jnp.abs(y)))


def _expm1(x, u):
    one = u == 1.0
    k = (u - 1.0) * x / jnp.where(one, 1.0, jnp.log(u))
    return jnp.where(one, x, jnp.where(x < -1.0, u - 1.0, k))


def _b_tile_index(reverse, n_lat_tiles, i):
    lat = (n_lat_tiles - i) if reverse else (i - 1)
    return jnp.where(i == 0, n_lat_tiles, lat)


def _b_scan_kernel(reverse, n_lat_tiles, *refs):
    if reverse:
        (xc_ref, xp_ref, xn_ref, cw_ref, cb_ref, lam_ref, wa_ref, ba_ref, wx_ref, bx_ref,
         hf_ref, gate_ref, out_ref, carry_ref) = refs
    else:
        (xc_ref, xp_ref, xn_ref, cw_ref, cb_ref, lam_ref, wa_ref, ba_ref, wx_ref, bx_ref,
         out_ref, carry_ref) = refs
    i = pl.program_id(0)
    tile = _b_tile_index(reverse, n_lat_tiles, i)

    @pl.when(i == 0)
    def _():
        carry_ref[...] = jnp.zeros(carry_ref.shape, F32)

    prev_ok = jnp.logical_and(tile >= 1, tile <= n_lat_tiles - 1)
    next_ok = tile <= n_lat_tiles - 2
    u = xc_ref[...]
    n = u.shape[0]
    row = lax.broadcasted_iota(jnp.int32, (n, 1), 0)
    pm = jnp.where(prev_ok, xp_ref[SUBLANE - 1:SUBLANE, :], 0.0)
    n0 = jnp.where(next_ok, xn_ref[0:1, :], 0.0)
    n1 = jnp.where(next_ok, xn_ref[1:2, :], 0.0)
    um1 = jnp.where(row == 0, pm, pltpu.roll(u, 1, 0))
    up1 = jnp.where(row == n - 1, n0, pltpu.roll(u, n - 1, 0))
    up2 = jnp.where(row == n - 2, n0, jnp.where(row == n - 1, n1, pltpu.roll(u, n - 2, 0)))
    cw = cw_ref[...]
    xb = cb_ref[...] + (cw[0:1] * um1 + cw[1:2] * u + cw[2:3] * up1 + cw[3:4] * up2)

    xb16 = xb.astype(BF16)
    rs, gs = [], []
    for nblk in range(B_BLOCKS):
        sl = slice(nblk * B_BLOCK_W, (nblk + 1) * B_BLOCK_W)
        rs.append(_dot(xb16[:, sl], wa_ref[nblk]))
        gs.append(_dot(xb16[:, sl], wx_ref[nblk]))
    r = jax.nn.sigmoid(jnp.concatenate(rs, axis=1) + ba_ref[...])
    gi = jax.nn.sigmoid(jnp.concatenate(gs, axis=1) + bx_ref[...])
    log_a = (-B_LRU_C * r) * _softplus(-lam_ref[...])
    a = jnp.exp(log_a)
    b = jnp.sqrt(-_expm1(2.0 * log_a, a * a)) * (gi * xb)

    sub = row % SUBLANE
    s = 1
    while s < SUBLANE:
        if reverse:
            shift, valid = n - s, sub < SUBLANE - s
        else:
            shift, valid = s, sub >= s
        a_sh = pltpu.roll(a, shift, 0)
        b_sh = pltpu.roll(b, shift, 0)
        b = jnp.where(valid, a * b_sh + b, b)
        a = jnp.where(valid, a * a_sh, a)
        s *= 2
    carry = carry_ref[0:1, :]
    n_groups = n // SUBLANE
    hs = [None] * n_groups
    for k in range(n_groups):
        gidx = n_groups - 1 - k if reverse else k
        rows = slice(gidx * SUBLANE, (gidx + 1) * SUBLANE)
        hg = a[rows] * carry + b[rows]
        carry = hg[0:1, :] if reverse else hg[SUBLANE - 1:SUBLANE, :]
        hs[gidx] = hg
    carry_ref[0:1, :] = carry
    h = jnp.concatenate(hs, axis=0)
    if reverse:
        out_ref[...] = ((hf_ref[...] + h) * gate_ref[...]).astype(out_ref.dtype)
    else:
        out_ref[...] = h


def _b_scan(reverse, xpre, conv_w, conv_b, lam, w_a, b_a, w_x, b_x, n_lat_tiles, hf=None, gate=None):
    n_tiles = xpre.shape[0] // ROW_TILE
    per8 = ROW_TILE // SUBLANE
    last8 = xpre.shape[0] // SUBLANE - 1
    tile = functools.partial(_b_tile_index, reverse, n_lat_tiles)
    cur = lambda i: (tile(i), 0)
    prev8 = lambda i: (jnp.maximum(tile(i) * per8 - 1, 0), 0)
    next8 = lambda i: (jnp.minimum((tile(i) + 1) * per8, last8), 0)
    w = B_WIDTH
    in_specs = [
        pl.BlockSpec((ROW_TILE, w), cur),
        pl.BlockSpec((SUBLANE, w), prev8),
        pl.BlockSpec((SUBLANE, w), next8),
        _resident(conv_w.shape), _resident(conv_b.shape), _resident(lam.shape),
        _resident(w_a.shape), _resident(b_a.shape), _resident(w_x.shape), _resident(b_x.shape),
    ]
    args = [xpre, xpre, xpre, conv_w, conv_b, lam, w_a, b_a, w_x, b_x]
    if reverse:
        in_specs += [pl.BlockSpec((ROW_TILE, w), cur), pl.BlockSpec((ROW_TILE, w), cur)]
        args += [hf, gate]
    return pl.pallas_call(
        functools.partial(_b_scan_kernel, reverse, n_lat_tiles),
        grid=(n_tiles,),
        in_specs=in_specs,
        out_specs=pl.BlockSpec((ROW_TILE, w), cur),
        out_shape=jax.ShapeDtypeStruct(xpre.shape, BF16 if reverse else F32),
        scratch_shapes=[pltpu.VMEM((SUBLANE, w), F32)],
        compiler_params=_params("arbitrary"),
        name="b_scan_rev" if reverse else "b_scan_fwd",
    )(*args)


C_IN_PAD = C_Q_RANK + C_KV_RANK + LANE
C_QR_PAD = C_HEADS * LANE


def _c_proj_kernel(x_ref, mods_ref, g_ref, win_ref, gq_ref, gkv_ref, wuq_ref, wuk_ref, wuvt_ref, cos_ref, sin_ref,
                   qn_ref, qr_ref, kn_ref, kr_ref, vt_ref):
    m = mods_ref[0]
    h = _norm_mod(x_ref[...], g_ref[0:1], m[0:1], m[1:2]).astype(BF16)
    z = _dot(h, win_ref[...])
    cq = _rms(z[:, :C_Q_RANK], gq_ref[...]).astype(BF16)
    ckv = _rms(z[:, C_Q_RANK:C_Q_RANK + C_KV_RANK], gkv_ref[...]).astype(BF16)
    cos = cos_ref[...]
    sin = sin_ref[...]
    half = C_ROPE // 4
    kr_ref[...] = _rope(z[:, C_Q_RANK + C_KV_RANK:], cos, sin, half).astype(BF16)
    q = _dot(cq, wuq_ref[...]) * C_QSCALE
    qn_ref[...] = q[:, :C_HEADS * C_NOPE].astype(BF16)
    for hd in range(C_HEADS):
        lo = C_HEADS * C_NOPE + hd * LANE
        qr_ref[:, hd * LANE:(hd + 1) * LANE] = _rope(q[:, lo:lo + LANE], cos, sin, half).astype(BF16)
    kn_ref[...] = _dot(ckv, wuk_ref[...]).astype(BF16)
    vt_ref[...] = _dot_nt(wuvt_ref[...], ckv).astype(BF16)


def _c_proj(xa, mods, g, w_in, g_q, g_kv, w_uq, w_uk, w_uvt, cos, sin, n_lat_tiles):
    n_rows = xa.shape[0]
    n_tiles = n_rows // ROW_TILE
    row = lambda i: (i, 0)
    hw = C_HEADS * C_NOPE
    return pl.pallas_call(
        _c_proj_kernel,
        grid=(n_tiles,),
        in_specs=[
            pl.BlockSpec((ROW_TILE, D_MODEL), row),
            _mods_spec(n_lat_tiles),
            _resident(g.shape), _resident(w_in.shape), _resident(g_q.shape), _resident(g_kv.shape),
            _resident(w_uq.shape), _resident(w_uk.shape), _resident(w_uvt.shape),
            pl.BlockSpec((ROW_TILE, LANE), row),
            pl.BlockSpec((ROW_TILE, LANE), row),
        ],
        out_specs=[
            pl.BlockSpec((ROW_TILE, hw), row),
            pl.BlockSpec((ROW_TILE, C_QR_PAD), row),
            pl.BlockSpec((ROW_TILE, hw), row),
            pl.BlockSpec((ROW_TILE, LANE), row),
            pl.BlockSpec((C_HEADS * C_VDIM, ROW_TILE), lambda i: (0, i)),
        ],
        out_shape=[
            jax.ShapeDtypeStruct((n_rows, hw), BF16),
            jax.ShapeDtypeStruct((n_rows, C_QR_PAD), BF16),
            jax.ShapeDtypeStruct((n_rows, hw), BF16),
            jax.ShapeDtypeStruct((n_rows, LANE), BF16),
            jax.ShapeDtypeStruct((C_HEADS * C_VDIM, n_rows), BF16),
        ],
        compiler_params=_params("parallel"),
        name="c_proj",
    )(xa, mods, g, w_in, g_q, g_kv, w_uq, w_uk, w_uvt, cos, sin)


def _c_scores(hd, qn_ref, qr_ref, kn_ref, kr_ref):
    hs = slice(hd * LANE, (hd + 1) * LANE)
    qcat = jnp.concatenate([qn_ref[:, hs], qr_ref[:, hs]], axis=1)
    kcat = jnp.concatenate([kn_ref[:, hs], kr_ref[...]], axis=1)
    return _dot_nt(kcat, qcat)


def _c_key_block_exact(qn_ref, qr_ref, kn_ref, kr_ref, vt_ref, m_ref, l_ref, acc_ref):
    for hd in range(C_HEADS):
        hs = slice(hd * LANE, (hd + 1) * LANE)
        st = _c_scores(hd, qn_ref, qr_ref, kn_ref, kr_ref)
        m_old = m_ref[hd:hd + 1, :]
        m_new = jnp.maximum(m_old, jnp.max(st, axis=0, keepdims=True))
        alpha = jnp.exp2(m_old - m_new)
        p = jnp.exp2(st - m_new)
        l_ref[hd:hd + 1, :] = alpha * l_ref[hd:hd + 1, :] + jnp.sum(p, axis=0, keepdims=True)
        acc_ref[hd] = alpha * acc_ref[hd] + _dot(vt_ref[hs, :], p.astype(BF16))
        m_ref[hd:hd + 1, :] = m_new


def _c_key_block_fixed_ref(qn_ref, qr_ref, kn_ref, kr_ref, vt_ref, m_ref, ps_ref, pv_ref):
    for hd in range(C_HEADS):
        hs = slice(hd * LANE, (hd + 1) * LANE)
        p = jnp.exp2(_c_scores(hd, qn_ref, qr_ref, kn_ref, kr_ref) - m_ref[hd:hd + 1, :])
        ps_ref[hd:hd + 1, :] = jnp.sum(p, axis=0, keepdims=True)
        pv_ref[hd] = _dot(vt_ref[hs, :], p.astype(BF16))


def _c_init(m_ref, l_ref, acc_ref):
    m_ref[...] = jnp.full(m_ref.shape, NEG, F32)
    l_ref[...] = jnp.zeros(l_ref.shape, F32)
    acc_ref[...] = jnp.zeros(acc_ref.shape, F32)


def _c_finish(l_ref, acc_ref, o_ref):
    for hd in range(C_HEADS):
        ot = acc_ref[hd] / l_ref[hd:hd + 1, :]
        o_ref[:, hd * C_VDIM:(hd + 1) * C_VDIM] = ot.T.astype(o_ref.dtype)


def _c_attn_lat_kernel(qn_ref, qr_ref, kn_ref, kr_ref, vt_ref, knx_ref, krx_ref, vtx_ref, o_ref,
                       m_ref, l_ref, acc_ref, ps_ref, pv_ref):
    j = pl.program_id(1)

    @pl.when(j == 0)
    def _():
        _c_init(m_ref, l_ref, acc_ref)
        _c_key_block_exact(qn_ref, qr_ref, knx_ref, krx_ref, vtx_ref, m_ref, l_ref, acc_ref)

    _c_key_block_fixed_ref(qn_ref, qr_ref, kn_ref, kr_ref, vt_ref, m_ref, ps_ref, pv_ref)
    safe = jnp.max(ps_ref[...]) < C_SAFE_SUM

    @pl.when(safe)
    def _():
        l_ref[...] = l_ref[...] + ps_ref[...]
        acc_ref[...] = acc_ref[...] + pv_ref[...]

    @pl.when(jnp.logical_not(safe))
    def _():
        _c_key_block_exact(qn_ref, qr_ref, kn_ref, kr_ref, vt_ref, m_ref, l_ref, acc_ref)

    @pl.when(j == pl.num_programs(1) - 1)
    def _():
        _c_finish(l_ref, acc_ref, o_ref)


def _c_attn_ctx_kernel(qn_ref, qr_ref, knx_ref, krx_ref, vtx_ref, o_ref, m_ref, l_ref, acc_ref):
    _c_init(m_ref, l_ref, acc_ref)
    _c_key_block_exact(qn_ref, qr_ref, knx_ref, krx_ref, vtx_ref, m_ref, l_ref, acc_ref)
    _c_finish(l_ref, acc_ref, o_ref)


def _c_scratch(tq, with_uncommitted):
    stats = [pltpu.VMEM((C_HEADS, tq), F32), pltpu.VMEM((C_HEADS, tq), F32), pltpu.VMEM((C_HEADS, C_VDIM, tq), F32)]
    return stats + (stats[1:] if with_uncommitted else [])


def _c_attn(qn, qr, kn, kr, vt, n_lat, n_ctx, with_ctx_queries):
    tq = min(512, n_lat)
    tk = min(2048, n_lat)
    hw = C_HEADS * C_NOPE
    cb = n_lat // n_ctx
    qrow = lambda i, j: (i, 0)
    krow = lambda i, j: (j, 0)
    o = pl.pallas_call(
        _c_attn_lat_kernel,
        grid=(n_lat // tq, n_lat // tk),
        in_specs=[
            pl.BlockSpec((tq, hw), qrow),
            pl.BlockSpec((tq, C_QR_PAD), qrow),
            pl.BlockSpec((tk, hw), krow),
            pl.BlockSpec((tk, LANE), krow),
            pl.BlockSpec((hw, tk), lambda i, j: (0, j)),
            pl.BlockSpec((n_ctx, hw), lambda i, j: (cb, 0)),
            pl.BlockSpec((n_ctx, LANE), lambda i, j: (cb, 0)),
            pl.BlockSpec((hw, n_ctx), lambda i, j: (0, cb)),
        ],
        out_specs=pl.BlockSpec((tq, hw), qrow),
        out_shape=jax.ShapeDtypeStruct((n_lat, hw), BF16),
        scratch_shapes=_c_scratch(tq, True),
        compiler_params=_params("parallel", "arbitrary"),
        name="c_attn_lat",
    )(qn, qr, kn, kr, vt, kn, kr, vt)
    if not with_ctx_queries:
        return o, o, 0
    o_ctx = pl.pallas_call(
        _c_attn_ctx_kernel,
        grid=(1,),
        in_specs=[
            pl.BlockSpec((n_ctx, hw), lambda i: (cb, 0)),
            pl.BlockSpec((n_ctx, C_QR_PAD), lambda i: (cb, 0)),
            pl.BlockSpec((n_ctx, hw), lambda i: (cb, 0)),
            pl.BlockSpec((n_ctx, LANE), lambda i: (cb, 0)),
            pl.BlockSpec((hw, n_ctx), lambda i: (0, cb)),
        ],
        out_specs=pl.BlockSpec((n_ctx, hw), lambda i: (0, 0)),
        out_shape=jax.ShapeDtypeStruct((n_ctx, hw), BF16),
        scratch_shapes=_c_scratch(n_ctx, False),
        compiler_params=_params("arbitrary"),
        name="c_attn_ctx",
    )(qn, qr, kn, kr, vt)
    return o, o_ctx, 0


def _rope_tables(n_lat, n_ctx, d_rot):
    n_rows = n_lat // GRID_W
    d_axis = d_rot // 2
    inv = 1.0 / (ROPE_BASE ** (jnp.arange(0, d_axis, 2, dtype=F32) / d_axis))
    ar = jnp.arange(n_rows, dtype=F32)[:, None] * inv
    ac = jnp.arange(GRID_W, dtype=F32)[:, None] * inv
    by_row = lambda t: jnp.repeat(t, GRID_W, axis=0)
    by_col = lambda t: jnp.tile(t, (n_rows, 1))
    cos_r, sin_r, cos_c, sin_c = by_row(jnp.cos(ar)), by_row(jnp.sin(ar)), by_col(jnp.cos(ac)), by_col(jnp.sin(ac))
    cos = jnp.concatenate([cos_r, cos_r, cos_c, cos_c], axis=1)
    sin = jnp.concatenate([-sin_r, sin_r, -sin_c, sin_c], axis=1)
    cos = jnp.pad(cos, ((0, 0), (0, LANE - d_rot)))
    sin = jnp.pad(sin, ((0, 0), (0, LANE - d_rot)))
    cos_ctx = jnp.pad(jnp.ones((n_ctx, d_rot), F32), ((0, 0), (0, LANE - d_rot)))
    return (jnp.concatenate([cos, cos_ctx], axis=0),
            jnp.concatenate([sin, jnp.zeros((n_ctx, LANE), F32)], axis=0))


def _pad_cols(w, width):
    return jnp.pad(w, ((0, 0), (0, width - w.shape[1])))


def kernel(x, c, ctx, c_ctx, ada_w, ada_b, norm_g, ffn_w_gu, ffn_w_down, a_w_qkv, a_sinks, a_w_o, b_w_in, b_conv_w, b_conv_b, b_lam, b_w_a, b_b_a, b_w_x, b_b_x, b_w_out, c_w_in, c_g_q, c_g_kv, c_w_uq, c_w_ukv, c_w_out):
    assert x.shape[0] == 1 and ctx.shape[0] == 1 and x.shape[2] == D_MODEL
    n_lat, n_ctx = x.shape[1], ctx.shape[1]
    assert n_ctx == ROW_TILE and n_lat % ROW_TILE == 0 and n_lat % GRID_W == 0
    n_lat_tiles = n_lat // ROW_TILE
    n_all_tiles = n_lat_tiles + 1

    x_stream = (x[0], ctx[0], 0)
    s_rows = jnp.zeros((SUBLANE, D_MODEL), F32).at[0].set(c[0]).at[1].set(c_ctx)
    mods_all = _ada(s_rows, ada_w, ada_b)[:, :2].reshape(DEPTH, 2, N_MOD, D_MODEL)
    cos_a, sin_a = _rope_tables(n_lat, n_ctx, A_HEAD_DIM)
    cos_c, sin_c = _rope_tables(n_lat, n_ctx, C_ROPE)

    for i in range(DEPTH):
        last = i == DEPTH - 1
        kind, j = i % N_MIXERS, i // N_MIXERS
        mods = mods_all[i]
        g = norm_g[i]
        if kind == 0:
            q, k, v = _a_proj(x_stream, mods, g, a_w_qkv[j].astype(BF16), cos_a, sin_a, n_lat_tiles)
            o = _a_attn(q, k, v, a_sinks[j], n_lat, n_ctx, not last)
            o_stream = _stream(o, n_lat_tiles) if not last else (o, o, 0)
            w_o = a_w_o[j]
        elif kind == 1:
            xa = x_stream[0]
            xpre, gate = _b_proj(xa, mods, g, b_w_in[j].astype(BF16), n_lat_tiles)
            scan_args = lambda d: (b_conv_w[j], b_conv_b[j][None], b_lam[j, d][None], b_w_a[j, d].astype(BF16),
                                   b_b_a[j, d][None], b_w_x[j, d].astype(BF16), b_b_x[j, d][None], n_lat_tiles)
            hf = _b_scan(False, xpre, *scan_args(0))
            o_stream = _stream(_b_scan(True, xpre, *scan_args(1), hf=hf, gate=gate), n_lat_tiles)
            w_o = b_w_out[j]
        else:
            xa = x_stream[0]
            w_in = c_w_in[j]
            w_in_p = _pad_cols(w_in, C_IN_PAD).astype(BF16)
            w_uq3 = c_w_uq[j].reshape(C_Q_RANK, C_HEADS, C_NOPE + C_ROPE)
            w_uq_p = jnp.concatenate(
                [w_uq3[..., :C_NOPE].reshape(C_Q_RANK, C_HEADS * C_NOPE),
                 jnp.pad(w_uq3[..., C_NOPE:], ((0, 0), (0, 0), (0, LANE - C_ROPE))).reshape(C_Q_RANK, C_QR_PAD)],
                axis=1).astype(BF16)
            w_ukv3 = c_w_ukv[j].reshape(C_KV_RANK, C_HEADS, C_NOPE + C_VDIM)
            w_uk = w_ukv3[..., :C_NOPE].reshape(C_KV_RANK, C_HEADS * C_NOPE).astype(BF16)
            w_uvt = w_ukv3[..., C_NOPE:].reshape(C_KV_RANK, C_HEADS * C_VDIM).T.astype(BF16)
            qn, qr, kn, kr, vt = _c_proj(xa, mods, g, w_in_p, c_g_q[j][None], c_g_kv[j][None], w_uq_p, w_uk, w_uvt,
                                         cos_c, sin_c, n_lat_tiles)
            o_stream = _c_attn(qn, qr, kn, kr, vt, n_lat, n_ctx, not last)
            w_o = c_w_out[j]
        xa = _post(o_stream, x_stream, mods, g, w_o.astype(BF16), ffn_w_gu[i].astype(BF16), ffn_w_down[i].astype(BF16),
                   n_lat_tiles, n_lat_tiles if last else n_all_tiles)
        x_stream = _stream(xa, n_lat_tiles)
    return xa[None, :n_lat]
```

```python
import functools
import math

import jax
import jax.numpy as jnp
from jax import lax
from jax.experimental import pallas as pl
from jax.experimental.pallas import tpu as pltpu

D_MODEL = 1024
DEPTH = 4
GRID_W = 64
N_MIXERS = 3
BLOCK = 128
ROPE_BASE = 10000.0
RMS_EPS = 1e-6
N_MOD = 6

A_HEADS = 8
A_KV_HEADS = 2
A_GROUP = A_HEADS // A_KV_HEADS
A_HEAD_DIM = D_MODEL // A_HEADS
A_WINDOW = 128
A_SCALE = A_HEAD_DIM ** -0.5
A_Q_W = A_HEADS * A_HEAD_DIM
A_KV_W = A_KV_HEADS * A_HEAD_DIM
A_QKV = A_Q_W + 2 * A_KV_W

B_WIDTH = D_MODEL
B_BLOCKS = 4
B_BLOCK_W = B_WIDTH // B_BLOCKS
B_CONV_W = 4
B_LRU_C = 8.0

C_HEADS = 8
C_NOPE = 128
C_ROPE = 64
C_VDIM = 128
C_Q_RANK = D_MODEL // 2
C_KV_RANK = D_MODEL // 4
C_SCALE = (C_NOPE + C_ROPE) ** -0.5
C_QSCALE = C_SCALE * math.log2(math.e)

FFN_HIDDEN = ((8 * D_MODEL + 3 * 256 - 1) // (3 * 256)) * 256

LANE = 128
SUBLANE = 8
ROW_TILE = 256
ROWS_PER_TILE = ROW_TILE // GRID_W
FFN_CHUNK = 256
C_SAFE_SUM = 2.0 ** 60
VMEM_LIMIT = 56 * 1024 * 1024
NEG = -1e30

BF16 = jnp.bfloat16
F32 = jnp.float32


def _dot(a, b):
    return jnp.dot(a, b, preferred_element_type=F32)


def _dot_nt(a, b):
    return lax.dot_general(a, b, (((1,), (1,)), ((), ())), preferred_element_type=F32)


def _resident(shape):
    nd = len(shape)
    return pl.BlockSpec(shape, lambda *_: (0,) * nd, pipeline_mode=pl.Buffered(1))


def _resident_layer(stack_shape, layer):
    nd = len(stack_shape) - 1
    return pl.BlockSpec((None,) + tuple(stack_shape[1:]), lambda *_: (layer,) + (0,) * nd, pipeline_mode=pl.Buffered(1))


def _params(*sem):
    return pltpu.CompilerParams(dimension_semantics=sem, vmem_limit_bytes=VMEM_LIMIT)


def _rms(x, g):
    ms = jnp.mean(x * x, axis=-1, keepdims=True)
    return x * lax.rsqrt(ms + RMS_EPS) * g


def _norm_mod(x, g, shift, scale):
    return _rms(x, g) * (1.0 + scale) + shift


def _rope(z, cos, sin, half):
    lane = lax.broadcasted_iota(jnp.int32, z.shape, 1)
    first = (lane % (2 * half)) < half
    partner = jnp.where(first, pltpu.roll(z, LANE - half, 1), pltpu.roll(z, half, 1))
    return z * cos + partner * sin


def _ada_kernel(s_ref, w_ref, b_ref, o_ref):
    s = s_ref[...]
    s = s * jax.nn.sigmoid(s)
    o_ref[0] = _dot(s.astype(BF16), w_ref[0].astype(BF16)) + b_ref[0]


def _ada(s_rows, ada_w, ada_b):
    tn = 1536
    depth, d, n = ada_w.shape
    return pl.pallas_call(
        _ada_kernel,
        grid=(depth, n // tn),
        in_specs=[
            pl.BlockSpec((SUBLANE, d), lambda l, j: (0, 0)),
            pl.BlockSpec((1, d, tn), lambda l, j: (l, 0, j)),
            pl.BlockSpec((1, 1, tn), lambda l, j: (l, 0, j)),
        ],
        out_specs=pl.BlockSpec((1, SUBLANE, tn), lambda l, j: (l, 0, j)),
        out_shape=jax.ShapeDtypeStruct((depth, SUBLANE, n), F32),
        compiler_params=_params("parallel", "parallel"),
        name="ada",
    )(s_rows, ada_w, ada_b.reshape(depth, 1, n))


def _mods_spec(n_lat_tiles):
    return pl.BlockSpec((1, N_MOD, D_MODEL), lambda i: (jnp.where(i >= n_lat_tiles, 1, 0), 0, 0))


def _stream(arr, n_lat_tiles):
    return arr, arr, n_lat_tiles


def _stream_specs(width, n_lat_tiles, ctx_tile):
    return [pl.BlockSpec((ROW_TILE, width), lambda i: (jnp.minimum(i, n_lat_tiles - 1), 0)),
            pl.BlockSpec((ROW_TILE, width), lambda i: (ctx_tile, 0))]


def _stream_tile(n_lat_tiles, lat_ref, ctx_ref):
    return jnp.where(pl.program_id(0) >= n_lat_tiles, ctx_ref[...], lat_ref[...])


def _post_kernel(n_lat_tiles, ol_ref, oc_ref, xl_ref, xc_ref, mods_ref, g_ref, wo_ref, wgu_ref, wd_ref, out_ref, act_ref):
    m = mods_ref[0]
    g = g_ref[...]
    y = _dot(_stream_tile(n_lat_tiles, ol_ref, oc_ref), wo_ref[...])
    x1 = _stream_tile(n_lat_tiles, xl_ref, xc_ref) + m[2:3] * _rms(y, g[1:2])
    h = _norm_mod(x1, g[2:3], m[3:4], m[4:5]).astype(BF16)
    for c in range(FFN_HIDDEN // FFN_CHUNK):
        lo = c * FFN_CHUNK
        zg = _dot(h, wgu_ref[:, lo:lo + FFN_CHUNK])
        zu = _dot(h, wgu_ref[:, FFN_HIDDEN + lo:FFN_HIDDEN + lo + FFN_CHUNK])
        act_ref[:, lo:lo + FFN_CHUNK] = (zg * jax.nn.sigmoid(zg) * zu).astype(BF16)
    f = _dot(act_ref[...], wd_ref[...])
    out_ref[...] = x1 + m[5:6] * _rms(f, g[3:4])


def _post(o_stream, x_stream, mods, g, w_o, w_gu, w_down, layer, n_lat_tiles, n_tiles):
    d = D_MODEL
    row = lambda i: (i, 0)
    o_lat, o_ctx, o_ctx_tile = o_stream
    x_lat, x_ctx, x_ctx_tile = x_stream
    return pl.pallas_call(
        functools.partial(_post_kernel, n_lat_tiles),
        grid=(n_tiles,),
        in_specs=_stream_specs(w_o.shape[0], n_lat_tiles, o_ctx_tile) + _stream_specs(d, n_lat_tiles, x_ctx_tile) + [
            _mods_spec(n_lat_tiles),
            _resident(g.shape),
            _resident(w_o.shape),
            _resident_layer(w_gu.shape, layer),
            _resident_layer(w_down.shape, layer),
        ],
        out_specs=pl.BlockSpec((ROW_TILE, d), row),
        out_shape=jax.ShapeDtypeStruct((n_tiles * ROW_TILE, d), F32),
        scratch_shapes=[pltpu.VMEM((ROW_TILE, FFN_HIDDEN), BF16)],
        compiler_params=_params("parallel"),
        name="post_ffn",
    )(o_lat, o_ctx, x_lat, x_ctx, mods, g, w_o, w_gu, w_down)


def _a_proj_kernel(n_lat_tiles, xl_ref, xc_ref, mods_ref, g_ref, w_ref, rc_ref, rs_ref, cc_ref, cs_ref, q_ref, k_ref, v_ref):
    m = mods_ref[0]
    h = _norm_mod(_stream_tile(n_lat_tiles, xl_ref, xc_ref), g_ref[0:1], m[0:1], m[1:2]).astype(BF16)
    z = _dot(h, w_ref[...])
    cos, sin = _rope_tile(pl.program_id(0) >= n_lat_tiles, rc_ref, rs_ref, cc_ref, cs_ref)
    half = A_HEAD_DIM // 4
    for hd in range(A_HEADS):
        lo = hd * A_HEAD_DIM
        q_ref[:, lo:lo + A_HEAD_DIM] = _rope(z[:, lo:lo + A_HEAD_DIM], cos, sin, half).astype(BF16)
    for hd in range(A_KV_HEADS):
        lo = hd * A_HEAD_DIM
        k_ref[:, lo:lo + A_HEAD_DIM] = _rope(z[:, A_Q_W + lo:A_Q_W + lo + A_HEAD_DIM], cos, sin, half).astype(BF16)
    v_ref[...] = z[:, A_Q_W + A_KV_W:].astype(BF16)


def _a_proj(x_stream, mods, g, w_qkv, rope, n_lat_tiles):
    n_tiles = n_lat_tiles + 1
    n_rows = n_tiles * ROW_TILE
    row = lambda i: (i, 0)
    x_lat, x_ctx, x_ctx_tile = x_stream
    return pl.pallas_call(
        functools.partial(_a_proj_kernel, n_lat_tiles),
        grid=(n_tiles,),
        in_specs=_stream_specs(D_MODEL, n_lat_tiles, x_ctx_tile) + [
            _mods_spec(n_lat_tiles),
            _resident(g.shape),
            _resident(w_qkv.shape),
        ] + _rope_specs(),
        out_specs=[
            pl.BlockSpec((ROW_TILE, A_Q_W), row),
            pl.BlockSpec((ROW_TILE, A_KV_W), row),
            pl.BlockSpec((ROW_TILE, A_KV_W), row),
        ],
        out_shape=[
            jax.ShapeDtypeStruct((n_rows, A_Q_W), BF16),
            jax.ShapeDtypeStruct((n_rows, A_KV_W), BF16),
            jax.ShapeDtypeStruct((n_rows, A_KV_W), BF16),
        ],
        compiler_params=_params("parallel"),
        name="a_proj",
    )(x_lat, x_ctx, mods, g, w_qkv, *rope)


def _a_attn_kernel(nb, sink_ref, q_ref, kp_ref, kc_ref, kn_ref, kx_ref, vp_ref, vc_ref, vn_ref, vx_ref, o_ref):
    i = pl.program_id(0)
    is_lat = i < nb
    prev_ok = jnp.logical_and(is_lat, i >= 1)
    next_ok = i < nb - 1
    rows = A_GROUP * BLOCK
    r = lax.broadcasted_iota(jnp.int32, (rows, BLOCK), 0) % BLOCK
    c = lax.broadcasted_iota(jnp.int32, (rows, BLOCK), 1)
    mask_p = jnp.logical_and(c >= r, prev_ok)
    mask_n = jnp.logical_and(c <= r, next_ok)
    grp = lax.broadcasted_iota(jnp.int32, (rows, 1), 0) // BLOCK
    for kh in range(A_KV_HEADS):
        ks = slice(kh * A_HEAD_DIM, (kh + 1) * A_HEAD_DIM)
        qg = jnp.concatenate(
            [q_ref[:, (kh * A_GROUP + gi) * A_HEAD_DIM:(kh * A_GROUP + gi + 1) * A_HEAD_DIM] for gi in range(A_GROUP)],
            axis=0)
        sink = jnp.zeros((rows, 1), F32)
        for gi in range(A_GROUP):
            sink = jnp.where(grp == gi, sink_ref[kh * A_GROUP + gi], sink)
        s_x = _dot_nt(qg, kx_ref[:, ks]) * A_SCALE
        s_p = jnp.where(mask_p, _dot_nt(qg, kp_ref[:, ks]) * A_SCALE, NEG)
        s_c = jnp.where(is_lat, _dot_nt(qg, kc_ref[:, ks]) * A_SCALE, NEG)
        s_n = jnp.where(mask_n, _dot_nt(qg, kn_ref[:, ks]) * A_SCALE, NEG)
        mx = jnp.maximum(jnp.max(s_x, axis=-1, keepdims=True), jnp.max(s_c, axis=-1, keepdims=True))
        mx = jnp.maximum(mx, jnp.maximum(jnp.max(s_p, axis=-1, keepdims=True), jnp.max(s_n, axis=-1, keepdims=True)))
        mx = jnp.maximum(mx, sink)
        e_x = jnp.exp(s_x - mx)
        e_p = jnp.exp(s_p - mx)
        e_c = jnp.exp(s_c - mx)
        e_n = jnp.exp(s_n - mx)
        den = (jnp.exp(sink - mx) + jnp.sum(e_x, axis=-1, keepdims=True) + jnp.sum(e_p, axis=-1, keepdims=True)
               + jnp.sum(e_c, axis=-1, keepdims=True) + jnp.sum(e_n, axis=-1, keepdims=True))
        o = (_dot(e_x.astype(BF16), vx_ref[:, ks]) + _dot(e_p.astype(BF16), vp_ref[:, ks])
             + _dot(e_c.astype(BF16), vc_ref[:, ks]) + _dot(e_n.astype(BF16), vn_ref[:, ks]))
        o = o / den
        for gi in range(A_GROUP):
            hd = kh * A_GROUP + gi
            o_ref[:, hd * A_HEAD_DIM:(hd + 1) * A_HEAD_DIM] = o[gi * BLOCK:(gi + 1) * BLOCK].astype(BF16)


def _a_attn(q, k, v, sinks, n_lat, n_ctx, with_ctx_queries):
    nb = n_lat // BLOCK
    n_blocks = nb + (n_ctx // BLOCK if with_ctx_queries else 0)
    last = (n_lat + n_ctx) // BLOCK - 1
    cur = lambda i: (i, 0)
    prev = lambda i: (jnp.maximum(i - 1, 0), 0)
    nxt = lambda i: (jnp.minimum(i + 1, last), 0)
    ctx = lambda i: (n_lat // n_ctx, 0)
    kv_specs = [
        pl.BlockSpec((BLOCK, A_KV_W), prev),
        pl.BlockSpec((BLOCK, A_KV_W), cur),
        pl.BlockSpec((BLOCK, A_KV_W), nxt),
        pl.BlockSpec((n_ctx, A_KV_W), ctx),
    ]
    return pl.pallas_call(
        functools.partial(_a_attn_kernel, nb),
        grid=(n_blocks,),
        in_specs=[pl.BlockSpec(memory_space=pltpu.SMEM), pl.BlockSpec((BLOCK, A_Q_W), cur)] + kv_specs + kv_specs,
        out_specs=pl.BlockSpec((BLOCK, A_Q_W), cur),
        out_shape=jax.ShapeDtypeStruct((n_blocks * BLOCK, A_Q_W), BF16),
        compiler_params=_params("parallel"),
        name="a_attn",
    )(sinks, q, k, k, k, k, v, v, v, v)


def _b_proj_kernel(x_ref, mods_ref, g_ref, w_ref, xpre_ref, gate_ref):
    m = mods_ref[0]
    h = _norm_mod(x_ref[...], g_ref[0:1], m[0:1], m[1:2]).astype(BF16)
    xpre_ref[...] = _dot(h, w_ref[:, :B_WIDTH])
    gate_ref[...] = jax.nn.gelu(_dot(h, w_ref[:, B_WIDTH:]), approximate=True)


def _b_proj(xa, mods, g, w_in, n_lat_tiles):
    n_tiles = xa.shape[0] // ROW_TILE
    row = lambda i: (i, 0)
    return pl.pallas_call(
        _b_proj_kernel,
        grid=(n_tiles,),
        in_specs=[
            pl.BlockSpec((ROW_TILE, D_MODEL), row),
            _mods_spec(n_lat_tiles),
            _resident(g.shape),
            _resident(w_in.shape),
        ],
        out_specs=[pl.BlockSpec((ROW_TILE, B_WIDTH), row), pl.BlockSpec((ROW_TILE, B_WIDTH), row)],
        out_shape=[jax.ShapeDtypeStruct((xa.shape[0], B_WIDTH), F32)] * 2,
        compiler_params=_params("parallel"),
        name="b_proj",
    )(xa, mods, g, w_in)


def _softplus(y):
    return jnp.maximum(y, 0.0) + jnp.log1p(jnp.exp(-jnp.abs(y)))


def _expm1(x, u):
    one = u == 1.0
    k = (u - 1.0) * x / jnp.where(one, 1.0, jnp.log(u))
    return jnp.where(one, x, jnp.where(x < -1.0, u - 1.0, k))


def _b_tile_index(reverse, n_lat_tiles, i):
    lat = (n_lat_tiles - i) if reverse else (i - 1)
    return jnp.where(i == 0, n_lat_tiles, lat)


def _b_scan_kernel(reverse, n_lat_tiles, *refs):
    if reverse:
        (xc_ref, xp_ref, xn_ref, cw_ref, cb_ref, lam_ref, wa_ref, ba_ref, wx_ref, bx_ref,
         hf_ref, gate_ref, out_ref, carry_ref) = refs
    else:
        (xc_ref, xp_ref, xn_ref, cw_ref, cb_ref, lam_ref, wa_ref, ba_ref, wx_ref, bx_ref,
         out_ref, carry_ref) = refs
    i = pl.program_id(0)
    tile = _b_tile_index(reverse, n_lat_tiles, i)

    @pl.when(i == 0)
    def _():
        carry_ref[...] = jnp.zeros(carry_ref.shape, F32)

    prev_ok = jnp.logical_and(tile >= 1, tile <= n_lat_tiles - 1)
    next_ok = tile <= n_lat_tiles - 2
    u = xc_ref[...]
    n = u.shape[0]
    row = lax.broadcasted_iota(jnp.int32, (n, 1), 0)
    pm = jnp.where(prev_ok, xp_ref[SUBLANE - 1:SUBLANE, :], 0.0)
    n0 = jnp.where(next_ok, xn_ref[0:1, :], 0.0)
    n1 = jnp.where(next_ok, xn_ref[1:2, :], 0.0)
    um1 = jnp.where(row == 0, pm, pltpu.roll(u, 1, 0))
    up1 = jnp.where(row == n - 1, n0, pltpu.roll(u, n - 1, 0))
    up2 = jnp.where(row == n - 2, n0, jnp.where(row == n - 1, n1, pltpu.roll(u, n - 2, 0)))
    cw = cw_ref[...]
    xb = cb_ref[...] + (cw[0:1] * um1 + cw[1:2] * u + cw[2:3] * up1 + cw[3:4] * up2)

    xb16 = xb.astype(BF16)
    rs, gs = [], []
    for nblk in range(B_BLOCKS):
        sl = slice(nblk * B_BLOCK_W, (nblk + 1) * B_BLOCK_W)
        rs.append(_dot(xb16[:, sl], wa_ref[nblk]))
        gs.append(_dot(xb16[:, sl], wx_ref[nblk]))
    r = jax.nn.sigmoid(jnp.concatenate(rs, axis=1) + ba_ref[...])
    gi = jax.nn.sigmoid(jnp.concatenate(gs, axis=1) + bx_ref[...])
    log_a = (-B_LRU_C * r) * _softplus(-lam_ref[...])
    a = jnp.exp(log_a)
    b = jnp.sqrt(-_expm1(2.0 * log_a, a * a)) * (gi * xb)

    sub = row % SUBLANE
    s = 1
    while s < SUBLANE:
        if reverse:
            shift, valid = n - s, sub < SUBLANE - s
        else:
            shift, valid = s, sub >= s
        a_sh = pltpu.roll(a, shift, 0)
        b_sh = pltpu.roll(b, shift, 0)
        b = jnp.where(valid, a * b_sh + b, b)
        a = jnp.where(valid, a * a_sh, a)
        s *= 2
    carry = carry_ref[0:1, :]
    n_groups = n // SUBLANE
    hs = [None] * n_groups
    for k in range(n_groups):
        gidx = n_groups - 1 - k if reverse else k
        rows = slice(gidx * SUBLANE, (gidx + 1) * SUBLANE)
        hg = a[rows] * carry + b[rows]
        carry = hg[0:1, :] if reverse else hg[SUBLANE - 1:SUBLANE, :]
        hs[gidx] = hg
    carry_ref[0:1, :] = carry
    h = jnp.concatenate(hs, axis=0)
    if reverse:
        out_ref[...] = ((hf_ref[...] + h) * gate_ref[...]).astype(out_ref.dtype)
    else:
        out_ref[...] = h


def _b_scan(reverse, xpre, conv_w, conv_b, lam, w_a, b_a, w_x, b_x, n_lat_tiles, hf=None, gate=None):
    n_tiles = xpre.shape[0] // ROW_TILE
    per8 = ROW_TILE // SUBLANE
    last8 = xpre.shape[0] // SUBLANE - 1
    tile = functools.partial(_b_tile_index, reverse, n_lat_tiles)
    cur = lambda i: (tile(i), 0)
    prev8 = lambda i: (jnp.maximum(tile(i) * per8 - 1, 0), 0)
    next8 = lambda i: (jnp.minimum((tile(i) + 1) * per8, last8), 0)
    w = B_WIDTH
    in_specs = [
        pl.BlockSpec((ROW_TILE, w), cur),
        pl.BlockSpec((SUBLANE, w), prev8),
        pl.BlockSpec((SUBLANE, w), next8),
        _resident(conv_w.shape), _resident(conv_b.shape), _resident(lam.shape),
        _resident(w_a.shape), _resident(b_a.shape), _resident(w_x.shape), _resident(b_x.shape),
    ]
    args = [xpre, xpre, xpre, conv_w, conv_b, lam, w_a, b_a, w_x, b_x]
    if reverse:
        in_specs += [pl.BlockSpec((ROW_TILE, w), cur), pl.BlockSpec((ROW_TILE, w), cur)]
        args += [hf, gate]
    return pl.pallas_call(
        functools.partial(_b_scan_kernel, reverse, n_lat_tiles),
        grid=(n_tiles,),
        in_specs=in_specs,
        out_specs=pl.BlockSpec((ROW_TILE, w), cur),
        out_shape=jax.ShapeDtypeStruct(xpre.shape, BF16 if reverse else F32),
        scratch_shapes=[pltpu.VMEM((SUBLANE, w), F32)],
        compiler_params=_params("arbitrary"),
        name="b_scan_rev" if reverse else "b_scan_fwd",
    )(*args)


C_IN_PAD = C_Q_RANK + C_KV_RANK + LANE
C_QR_PAD = C_HEADS * LANE


def _c_proj_kernel(n_lat_tiles, x_ref, mods_ref, g_ref, win_ref, gq_ref, gkv_ref, wuq_ref, wuk_ref, wuvt_ref,
                   rc_ref, rs_ref, cc_ref, cs_ref, qn_ref, qr_ref, kn_ref, kr_ref, vt_ref):
    m = mods_ref[0]
    h = _norm_mod(x_ref[...], g_ref[0:1], m[0:1], m[1:2]).astype(BF16)
    z = _dot(h, win_ref[...])
    cq = _rms(z[:, :C_Q_RANK], gq_ref[...]).astype(BF16)
    ckv = _rms(z[:, C_Q_RANK:C_Q_RANK + C_KV_RANK], gkv_ref[...]).astype(BF16)
    cos, sin = _rope_tile(pl.program_id(0) >= n_lat_tiles, rc_ref, rs_ref, cc_ref, cs_ref)
    half = C_ROPE // 4
    kr_ref[...] = _rope(z[:, C_Q_RANK + C_KV_RANK:], cos, sin, half).astype(BF16)
    q = _dot(cq, wuq_ref[...]) * C_QSCALE
    qn_ref[...] = q[:, :C_HEADS * C_NOPE].astype(BF16)
    for hd in range(C_HEADS):
        lo = C_HEADS * C_NOPE + hd * LANE
        qr_ref[:, hd * LANE:(hd + 1) * LANE] = _rope(q[:, lo:lo + LANE], cos, sin, half).astype(BF16)
    kn_ref[...] = _dot(ckv, wuk_ref[...]).astype(BF16)
    vt_ref[...] = _dot_nt(wuvt_ref[...], ckv).astype(BF16)


def _c_proj(xa, mods, g, w_in, g_q, g_kv, w_uq, w_uk, w_uvt, rope, n_lat_tiles):
    n_rows = xa.shape[0]
    n_tiles = n_rows // ROW_TILE
    row = lambda i: (i, 0)
    hw = C_HEADS * C_NOPE
    return pl.pallas_call(
        functools.partial(_c_proj_kernel, n_lat_tiles),
        grid=(n_tiles,),
        in_specs=[
            pl.BlockSpec((ROW_TILE, D_MODEL), row),
            _mods_spec(n_lat_tiles),
            _resident(g.shape), _resident(w_in.shape), _resident(g_q.shape), _resident(g_kv.shape),
            _resident(w_uq.shape), _resident(w_uk.shape), _resident(w_uvt.shape),
        ] + _rope_specs(),
        out_specs=[
            pl.BlockSpec((ROW_TILE, hw), row),
            pl.BlockSpec((ROW_TILE, C_QR_PAD), row),
            pl.BlockSpec((ROW_TILE, hw), row),
            pl.BlockSpec((ROW_TILE, LANE), row),
            pl.BlockSpec((C_HEADS * C_VDIM, ROW_TILE), lambda i: (0, i)),
        ],
        out_shape=[
            jax.ShapeDtypeStruct((n_rows, hw), BF16),
            jax.ShapeDtypeStruct((n_rows, C_QR_PAD), BF16),
            jax.ShapeDtypeStruct((n_rows, hw), BF16),
            jax.ShapeDtypeStruct((n_rows, LANE), BF16),
            jax.ShapeDtypeStruct((C_HEADS * C_VDIM, n_rows), BF16),
        ],
        compiler_params=_params("parallel"),
        name="c_proj",
    )(xa, mods, g, w_in, g_q, g_kv, w_uq, w_uk, w_uvt, *rope)


def _c_scores(hd, qn_ref, qr_ref, kn_ref, kr_ref):
    hs = slice(hd * LANE, (hd + 1) * LANE)
    qcat = jnp.concatenate([qn_ref[:, hs], qr_ref[:, hs]], axis=1)
    kcat = jnp.concatenate([kn_ref[:, hs], kr_ref[...]], axis=1)
    return _dot_nt(kcat, qcat)


def _c_key_block_exact(qn_ref, qr_ref, kn_ref, kr_ref, vt_ref, m_ref, l_ref, acc_ref):
    for hd in range(C_HEADS):
        hs = slice(hd * LANE, (hd + 1) * LANE)
        st = _c_scores(hd, qn_ref, qr_ref, kn_ref, kr_ref)
        m_old = m_ref[hd:hd + 1, :]
        m_new = jnp.maximum(m_old, jnp.max(st, axis=0, keepdims=True))
        alpha = jnp.exp2(m_old - m_new)
        p = jnp.exp2(st - m_new)
        l_ref[hd:hd + 1, :] = alpha * l_ref[hd:hd + 1, :] + jnp.sum(p, axis=0, keepdims=True)
        acc_ref[hd] = alpha * acc_ref[hd] + _dot(vt_ref[hs, :], p.astype(BF16))
        m_ref[hd:hd + 1, :] = m_new


def _c_key_block_fixed_ref(qn_ref, qr_ref, kn_ref, kr_ref, vt_ref, m_ref, ps_ref, pv_ref):
    for hd in range(C_HEADS):
        hs = slice(hd * LANE, (hd + 1) * LANE)
        p = jnp.exp2(_c_scores(hd, qn_ref, qr_ref, kn_ref, kr_ref) - m_ref[hd:hd + 1, :])
        ps_ref[hd:hd + 1, :] = jnp.sum(p, axis=0, keepdims=True)
        pv_ref[hd] = _dot(vt_ref[hs, :], p.astype(BF16))


def _c_init(m_ref, l_ref, acc_ref):
    m_ref[...] = jnp.full(m_ref.shape, NEG, F32)
    l_ref[...] = jnp.zeros(l_ref.shape, F32)
    acc_ref[...] = jnp.zeros(acc_ref.shape, F32)


def _c_finish(l_ref, acc_ref, o_ref):
    for hd in range(C_HEADS):
        ot = acc_ref[hd] / l_ref[hd:hd + 1, :]
        o_ref[:, hd * C_VDIM:(hd + 1) * C_VDIM] = ot.T.astype(o_ref.dtype)


def _c_attn_lat_kernel(qn_ref, qr_ref, kn_ref, kr_ref, vt_ref, knx_ref, krx_ref, vtx_ref, o_ref,
                       m_ref, l_ref, acc_ref, ps_ref, pv_ref):
    j = pl.program_id(1)

    @pl.when(j == 0)
    def _():
        _c_init(m_ref, l_ref, acc_ref)
        _c_key_block_exact(qn_ref, qr_ref, knx_ref, krx_ref, vtx_ref, m_ref, l_ref, acc_ref)

    _c_key_block_fixed_ref(qn_ref, qr_ref, kn_ref, kr_ref, vt_ref, m_ref, ps_ref, pv_ref)
    safe = jnp.max(ps_ref[...]) < C_SAFE_SUM

    @pl.when(safe)
    def _():
        l_ref[...] = l_ref[...] + ps_ref[...]
        acc_ref[...] = acc_ref[...] + pv_ref[...]

    @pl.when(jnp.logical_not(safe))
    def _():
        _c_key_block_exact(qn_ref, qr_ref, kn_ref, kr_ref, vt_ref, m_ref, l_ref, acc_ref)

    @pl.when(j == pl.num_programs(1) - 1)
    def _():
        _c_finish(l_ref, acc_ref, o_ref)


def _c_attn_ctx_kernel(qn_ref, qr_ref, knx_ref, krx_ref, vtx_ref, o_ref, m_ref, l_ref, acc_ref):
    _c_init(m_ref, l_ref, acc_ref)
    _c_key_block_exact(qn_ref, qr_ref, knx_ref, krx_ref, vtx_ref, m_ref, l_ref, acc_ref)
    _c_finish(l_ref, acc_ref, o_ref)


def _c_scratch(tq, with_uncommitted):
    stats = [pltpu.VMEM((C_HEADS, tq), F32), pltpu.VMEM((C_HEADS, tq), F32), pltpu.VMEM((C_HEADS, C_VDIM, tq), F32)]
    return stats + (stats[1:] if with_uncommitted else [])


def _c_attn(qn, qr, kn, kr, vt, n_lat, n_ctx, with_ctx_queries):
    tq = min(512, n_lat)
    tk = min(2048, n_lat)
    hw = C_HEADS * C_NOPE
    cb = n_lat // n_ctx
    qrow = lambda i, j: (i, 0)
    krow = lambda i, j: (j, 0)
    o = pl.pallas_call(
        _c_attn_lat_kernel,
        grid=(n_lat // tq, n_lat // tk),
        in_specs=[
            pl.BlockSpec((tq, hw), qrow),
            pl.BlockSpec((tq, C_QR_PAD), qrow),
            pl.BlockSpec((tk, hw), krow),
            pl.BlockSpec((tk, LANE), krow),
            pl.BlockSpec((hw, tk), lambda i, j: (0, j)),
            pl.BlockSpec((n_ctx, hw), lambda i, j: (cb, 0)),
            pl.BlockSpec((n_ctx, LANE), lambda i, j: (cb, 0)),
            pl.BlockSpec((hw, n_ctx), lambda i, j: (0, cb)),
        ],
        out_specs=pl.BlockSpec((tq, hw), qrow),
        out_shape=jax.ShapeDtypeStruct((n_lat, hw), BF16),
        scratch_shapes=_c_scratch(tq, True),
        compiler_params=_params("parallel", "arbitrary"),
        name="c_attn_lat",
    )(qn, qr, kn, kr, vt, kn, kr, vt)
    if not with_ctx_queries:
        return o, o, 0
    o_ctx = pl.pallas_call(
        _c_attn_ctx_kernel,
        grid=(1,),
        in_specs=[
            pl.BlockSpec((n_ctx, hw), lambda i: (cb, 0)),
            pl.BlockSpec((n_ctx, C_QR_PAD), lambda i: (cb, 0)),
            pl.BlockSpec((n_ctx, hw), lambda i: (cb, 0)),
            pl.BlockSpec((n_ctx, LANE), lambda i: (cb, 0)),
            pl.BlockSpec((hw, n_ctx), lambda i: (0, cb)),
        ],
        out_specs=pl.BlockSpec((n_ctx, hw), lambda i: (0, 0)),
        out_shape=jax.ShapeDtypeStruct((n_ctx, hw), BF16),
        scratch_shapes=_c_scratch(n_ctx, False),
        compiler_params=_params("arbitrary"),
        name="c_attn_ctx",
    )(qn, qr, kn, kr, vt)
    return o, o_ctx, 0


def _rope_tables(n_lat, d_rot):
    n_rows = n_lat // GRID_W
    d_axis = d_rot // 2
    quarter = d_rot // 4
    inv = 1.0 / (ROPE_BASE ** (jnp.arange(0, d_axis, 2, dtype=F32) / d_axis))
    ar = jnp.arange(n_rows, dtype=F32)[:, None] * inv
    ac = jnp.arange(GRID_W, dtype=F32)[:, None] * inv

    def rows(a, b):
        t = jnp.pad(jnp.concatenate([a, b], axis=1), ((0, 0), (0, LANE - 2 * quarter)))
        t = t.reshape(n_rows // ROWS_PER_TILE, ROWS_PER_TILE, LANE)
        return jnp.pad(t, ((0, 1), (0, SUBLANE - ROWS_PER_TILE), (0, 0)))

    def cols(a, b):
        t = jnp.pad(jnp.concatenate([a, b], axis=1), ((0, 0), (2 * quarter, LANE - 4 * quarter)))
        return jnp.tile(t, (ROWS_PER_TILE, 1))

    cr, sr, cc, sc = jnp.cos(ar), jnp.sin(ar), jnp.cos(ac), jnp.sin(ac)
    return rows(cr, cr), rows(-sr, sr), cols(cc, cc), cols(-sc, sc)


def _rope_specs():
    row = pl.BlockSpec((1, SUBLANE, LANE), lambda i: (i, 0, 0))
    return [row, row, _resident((ROW_TILE, LANE)), _resident((ROW_TILE, LANE))]


def _rope_tile(is_ctx, row_cos_ref, row_sin_ref, col_cos_ref, col_sin_ref):
    def spread(ref):
        t = ref[0]
        return jnp.concatenate([jnp.broadcast_to(t[r:r + 1, :], (GRID_W, LANE)) for r in range(ROWS_PER_TILE)], axis=0)
    cos = jnp.where(is_ctx, 1.0, spread(row_cos_ref) + col_cos_ref[...])
    sin = jnp.where(is_ctx, 0.0, spread(row_sin_ref) + col_sin_ref[...])
    return cos, sin


def _pad_cols(w, width):
    return jnp.pad(w, ((0, 0), (0, width - w.shape[1])))


def kernel(x, c, ctx, c_ctx, ada_w, ada_b, norm_g, ffn_w_gu, ffn_w_down, a_w_qkv, a_sinks, a_w_o, b_w_in, b_conv_w, b_conv_b, b_lam, b_w_a, b_b_a, b_w_x, b_b_x, b_w_out, c_w_in, c_g_q, c_g_kv, c_w_uq, c_w_ukv, c_w_out):
    assert x.shape[0] == 1 and ctx.shape[0] == 1 and x.shape[2] == D_MODEL
    n_lat, n_ctx = x.shape[1], ctx.shape[1]
    assert n_ctx == ROW_TILE and n_lat % ROW_TILE == 0 and n_lat % GRID_W == 0
    n_lat_tiles = n_lat // ROW_TILE
    n_all_tiles = n_lat_tiles + 1

    x_stream = (x[0], ctx[0], 0)
    s_rows = jnp.zeros((SUBLANE, D_MODEL), F32).at[0].set(c[0]).at[1].set(c_ctx)
    mods_all = _ada(s_rows, ada_w, ada_b)[:, :2].reshape(DEPTH, 2, N_MOD, D_MODEL)
    w_gu_all = ffn_w_gu.astype(BF16)
    w_down_all = ffn_w_down.astype(BF16)
    rope_a = _rope_tables(n_lat, A_HEAD_DIM)
    rope_c = _rope_tables(n_lat, C_ROPE)

    for i in range(DEPTH):
        last = i == DEPTH - 1
        kind, j = i % N_MIXERS, i // N_MIXERS
        mods = mods_all[i]
        g = norm_g[i]
        if kind == 0:
            q, k, v = _a_proj(x_stream, mods, g, a_w_qkv[j].astype(BF16), rope_a, n_lat_tiles)
            o = _a_attn(q, k, v, a_sinks[j], n_lat, n_ctx, not last)
            o_stream = _stream(o, n_lat_tiles) if not last else (o, o, 0)
            w_o = a_w_o[j]
        elif kind == 1:
            xa = x_stream[0]
            xpre, gate = _b_proj(xa, mods, g, b_w_in[j].astype(BF16), n_lat_tiles)
            scan_args = lambda d: (b_conv_w[j], b_conv_b[j][None], b_lam[j, d][None], b_w_a[j, d].astype(BF16),
                                   b_b_a[j, d][None], b_w_x[j, d].astype(BF16), b_b_x[j, d][None], n_lat_tiles)
            hf = _b_scan(False, xpre, *scan_args(0))
            o_stream = _stream(_b_scan(True, xpre, *scan_args(1), hf=hf, gate=gate), n_lat_tiles)
            w_o = b_w_out[j]
        else:
            xa = x_stream[0]
            w_in = c_w_in[j]
            w_in_p = _pad_cols(w_in, C_IN_PAD).astype(BF16)
            w_uq3 = c_w_uq[j].reshape(C_Q_RANK, C_HEADS, C_NOPE + C_ROPE)
            w_uq_p = jnp.concatenate(
                [w_uq3[..., :C_NOPE].reshape(C_Q_RANK, C_HEADS * C_NOPE),
                 jnp.pad(w_uq3[..., C_NOPE:], ((0, 0), (0, 0), (0, LANE - C_ROPE))).reshape(C_Q_RANK, C_QR_PAD)],
                axis=1).astype(BF16)
            w_ukv3 = c_w_ukv[j].reshape(C_KV_RANK, C_HEADS, C_NOPE + C_VDIM)
            w_uk = w_ukv3[..., :C_NOPE].reshape(C_KV_RANK, C_HEADS * C_NOPE).astype(BF16)
            w_uvt = w_ukv3[..., C_NOPE:].reshape(C_KV_RANK, C_HEADS * C_VDIM).T.astype(BF16)
            qn, qr, kn, kr, vt = _c_proj(xa, mods, g, w_in_p, c_g_q[j][None], c_g_kv[j][None], w_uq_p, w_uk, w_uvt,
                                         rope_c, n_lat_tiles)
            o_stream = _c_attn(qn, qr, kn, kr, vt, n_lat, n_ctx, not last)
            w_o = c_w_out[j]
        xa = _post(o_stream, x_stream, mods, g, w_o.astype(BF16), w_gu_all, w_down_all, i, n_lat_tiles,
                   n_lat_tiles if last else n_all_tiles)
        x_stream = _stream(xa, n_lat_tiles)
    return xa[None, :n_lat]
```

```python
import functools
import math

import jax
import jax.numpy as jnp
from jax import lax
from jax.experimental import pallas as pl
from jax.experimental.pallas import tpu as pltpu

D_MODEL = 1024
DEPTH = 4
GRID_W = 64
N_MIXERS = 3
BLOCK = 128
ROPE_BASE = 10000.0
RMS_EPS = 1e-6
N_MOD = 6

A_HEADS = 8
A_KV_HEADS = 2
A_GROUP = A_HEADS // A_KV_HEADS
A_HEAD_DIM = D_MODEL // A_HEADS
A_WINDOW = 128
A_SCALE = A_HEAD_DIM ** -0.5
LOG2E = math.log2(math.e)
A_QSCALE = A_SCALE * LOG2E
A_Q_W = A_HEADS * A_HEAD_DIM
A_KV_W = A_KV_HEADS * A_HEAD_DIM
A_QKV = A_Q_W + 2 * A_KV_W

B_WIDTH = D_MODEL
B_BLOCKS = 4
B_BLOCK_W = B_WIDTH // B_BLOCKS
B_CONV_W = 4
B_LRU_C = 8.0

C_HEADS = 8
C_NOPE = 128
C_ROPE = 64
C_VDIM = 128
C_Q_RANK = D_MODEL // 2
C_KV_RANK = D_MODEL // 4
C_SCALE = (C_NOPE + C_ROPE) ** -0.5
C_QSCALE = C_SCALE * LOG2E

FFN_HIDDEN = ((8 * D_MODEL + 3 * 256 - 1) // (3 * 256)) * 256

LANE = 128
SUBLANE = 8
ROW_TILE = 256
ROWS_PER_TILE = ROW_TILE // GRID_W
FFN_CHUNK = 256
C_SAFE_SUM = 2.0 ** 60
VMEM_LIMIT = 56 * 1024 * 1024
NEG = -1e30

BF16 = jnp.bfloat16
F32 = jnp.float32


def _dot(a, b):
    return jnp.dot(a, b, preferred_element_type=F32)


def _dot_nt(a, b):
    return lax.dot_general(a, b, (((1,), (1,)), ((), ())), preferred_element_type=F32)


def _resident(shape):
    nd = len(shape)
    return pl.BlockSpec(shape, lambda *_: (0,) * nd, pipeline_mode=pl.Buffered(1))


def _resident_layer(stack_shape, layer):
    nd = len(stack_shape) - 1
    return pl.BlockSpec((None,) + tuple(stack_shape[1:]), lambda *_: (layer,) + (0,) * nd, pipeline_mode=pl.Buffered(1))


def _params(*sem):
    return pltpu.CompilerParams(dimension_semantics=sem, vmem_limit_bytes=VMEM_LIMIT)


def _rms(x, g):
    ms = jnp.mean(x * x, axis=-1, keepdims=True)
    return x * lax.rsqrt(ms + RMS_EPS) * g


def _norm_mod(x, g, shift, scale):
    return _rms(x, g) * (1.0 + scale) + shift


def _rope(z, cos, sin, half):
    lane = lax.broadcasted_iota(jnp.int32, z.shape, 1)
    first = (lane % (2 * half)) < half
    partner = jnp.where(first, pltpu.roll(z, LANE - half, 1), pltpu.roll(z, half, 1))
    return z * cos + partner * sin


def _ada_kernel(s_ref, w_ref, b_ref, o_ref):
    s = s_ref[...]
    s = s * jax.nn.sigmoid(s)
    o_ref[0] = _dot(s.astype(BF16), w_ref[0].astype(BF16)) + b_ref[0]


def _ada(s_rows, ada_w, ada_b):
    tn = 1536
    depth, d, n = ada_w.shape
    return pl.pallas_call(
        _ada_kernel,
        grid=(depth, n // tn),
        in_specs=[
            pl.BlockSpec((SUBLANE, d), lambda l, j: (0, 0)),
            pl.BlockSpec((1, d, tn), lambda l, j: (l, 0, j)),
            pl.BlockSpec((1, 1, tn), lambda l, j: (l, 0, j)),
        ],
        out_specs=pl.BlockSpec((1, SUBLANE, tn), lambda l, j: (l, 0, j)),
        out_shape=jax.ShapeDtypeStruct((depth, SUBLANE, n), F32),
        compiler_params=_params("parallel", "parallel"),
        name="ada",
    )(s_rows, ada_w, ada_b.reshape(depth, 1, n))


def _mods_spec(n_lat_tiles):
    return pl.BlockSpec((1, N_MOD, D_MODEL), lambda i: (jnp.where(i >= n_lat_tiles, 1, 0), 0, 0))


def _stream(arr, n_lat_tiles):
    return arr, arr, n_lat_tiles


def _stream_specs(width, n_lat_tiles, ctx_tile):
    return [pl.BlockSpec((ROW_TILE, width), lambda i: (jnp.minimum(i, n_lat_tiles - 1), 0)),
            pl.BlockSpec((ROW_TILE, width), lambda i: (ctx_tile, 0))]


def _stream_tile(n_lat_tiles, lat_ref, ctx_ref):
    return jnp.where(pl.program_id(0) >= n_lat_tiles, ctx_ref[...], lat_ref[...])


def _post_kernel(n_lat_tiles, ol_ref, oc_ref, xl_ref, xc_ref, mods_ref, g_ref, wo_ref, wgu_ref, wd_ref, out_ref, act_ref):
    m = mods_ref[0]
    g = g_ref[...]
    y = _dot(_stream_tile(n_lat_tiles, ol_ref, oc_ref), wo_ref[...])
    x1 = _stream_tile(n_lat_tiles, xl_ref, xc_ref) + m[2:3] * _rms(y, g[1:2])
    h = _norm_mod(x1, g[2:3], m[3:4], m[4:5]).astype(BF16)
    for c in range(FFN_HIDDEN // FFN_CHUNK):
        lo = c * FFN_CHUNK
        zg = _dot(h, wgu_ref[:, lo:lo + FFN_CHUNK])
        zu = _dot(h, wgu_ref[:, FFN_HIDDEN + lo:FFN_HIDDEN + lo + FFN_CHUNK])
        act_ref[:, lo:lo + FFN_CHUNK] = (zg * jax.nn.sigmoid(zg) * zu).astype(BF16)
    f = _dot(act_ref[...], wd_ref[...])
    out_ref[...] = x1 + m[5:6] * _rms(f, g[3:4])


def _post(o_stream, x_stream, mods, g, w_o, w_gu, w_down, layer, n_lat_tiles, n_tiles):
    d = D_MODEL
    row = lambda i: (i, 0)
    o_lat, o_ctx, o_ctx_tile = o_stream
    x_lat, x_ctx, x_ctx_tile = x_stream
    return pl.pallas_call(
        functools.partial(_post_kernel, n_lat_tiles),
        grid=(n_tiles,),
        in_specs=_stream_specs(w_o.shape[0], n_lat_tiles, o_ctx_tile) + _stream_specs(d, n_lat_tiles, x_ctx_tile) + [
            _mods_spec(n_lat_tiles),
            _resident(g.shape),
            _resident(w_o.shape),
            _resident_layer(w_gu.shape, layer),
            _resident_layer(w_down.shape, layer),
        ],
        out_specs=pl.BlockSpec((ROW_TILE, d), row),
        out_shape=jax.ShapeDtypeStruct((n_tiles * ROW_TILE, d), F32),
        scratch_shapes=[pltpu.VMEM((ROW_TILE, FFN_HIDDEN), BF16)],
        compiler_params=_params("parallel"),
        name="post_ffn",
    )(o_lat, o_ctx, x_lat, x_ctx, mods, g, w_o, w_gu, w_down)


def _a_proj_kernel(n_lat_tiles, xl_ref, xc_ref, mods_ref, g_ref, w_ref, rc_ref, rs_ref, cc_ref, cs_ref, q_ref, k_ref, v_ref):
    m = mods_ref[0]
    h = _norm_mod(_stream_tile(n_lat_tiles, xl_ref, xc_ref), g_ref[0:1], m[0:1], m[1:2]).astype(BF16)
    z = _dot(h, w_ref[...])
    cos, sin = _rope_tile(pl.program_id(0) >= n_lat_tiles, rc_ref, rs_ref, cc_ref, cs_ref)
    half = A_HEAD_DIM // 4
    for hd in range(A_HEADS):
        lo = hd * A_HEAD_DIM
        q_ref[:, lo:lo + A_HEAD_DIM] = (_rope(z[:, lo:lo + A_HEAD_DIM], cos, sin, half) * A_QSCALE).astype(BF16)
    for hd in range(A_KV_HEADS):
        lo = hd * A_HEAD_DIM
        k_ref[:, lo:lo + A_HEAD_DIM] = _rope(z[:, A_Q_W + lo:A_Q_W + lo + A_HEAD_DIM], cos, sin, half).astype(BF16)
    v_ref[...] = z[:, A_Q_W + A_KV_W:].astype(BF16)


def _a_proj(x_stream, mods, g, w_qkv, rope, n_lat_tiles):
    n_tiles = n_lat_tiles + 1
    n_rows = n_tiles * ROW_TILE
    row = lambda i: (i, 0)
    x_lat, x_ctx, x_ctx_tile = x_stream
    return pl.pallas_call(
        functools.partial(_a_proj_kernel, n_lat_tiles),
        grid=(n_tiles,),
        in_specs=_stream_specs(D_MODEL, n_lat_tiles, x_ctx_tile) + [
            _mods_spec(n_lat_tiles),
            _resident(g.shape),
            _resident(w_qkv.shape),
        ] + _rope_specs(),
        out_specs=[
            pl.BlockSpec((ROW_TILE, A_Q_W), row),
            pl.BlockSpec((ROW_TILE, A_KV_W), row),
            pl.BlockSpec((ROW_TILE, A_KV_W), row),
        ],
        out_shape=[
            jax.ShapeDtypeStruct((n_rows, A_Q_W), BF16),
            jax.ShapeDtypeStruct((n_rows, A_KV_W), BF16),
            jax.ShapeDtypeStruct((n_rows, A_KV_W), BF16),
        ],
        compiler_params=_params("parallel"),
        name="a_proj",
    )(x_lat, x_ctx, mods, g, w_qkv, *rope)


def _a_attn_kernel(n_lat_tiles, sink_ref, q_ref, kp_ref, kc_ref, kn_ref, kx_ref, vp_ref, vc_ref, vn_ref, vx_ref, o_ref):
    i = pl.program_id(0)
    is_lat = i < n_lat_tiles
    prev_ok = jnp.logical_and(is_lat, i >= 1)
    next_ok = i < n_lat_tiles - 1
    rows = A_GROUP * ROW_TILE
    r = lax.broadcasted_iota(jnp.int32, (rows, BLOCK), 0) % ROW_TILE
    c = lax.broadcasted_iota(jnp.int32, (rows, BLOCK), 1)
    r2 = lax.broadcasted_iota(jnp.int32, (rows, ROW_TILE), 0) % ROW_TILE
    c2 = lax.broadcasted_iota(jnp.int32, (rows, ROW_TILE), 1)
    mask_p = jnp.logical_and(c >= r, prev_ok)
    mask_n = jnp.logical_and(c <= r - A_WINDOW, next_ok)
    mask_c = jnp.logical_and(jnp.abs(r2 - c2) <= A_WINDOW, is_lat)
    grp = lax.broadcasted_iota(jnp.int32, (rows, 1), 0) // ROW_TILE
    for kh in range(A_KV_HEADS):
        ks = slice(kh * A_HEAD_DIM, (kh + 1) * A_HEAD_DIM)
        qg = jnp.concatenate(
            [q_ref[:, (kh * A_GROUP + gi) * A_HEAD_DIM:(kh * A_GROUP + gi + 1) * A_HEAD_DIM] for gi in range(A_GROUP)],
            axis=0)
        sink = jnp.zeros((rows, 1), F32)
        for gi in range(A_GROUP):
            sink = jnp.where(grp == gi, sink_ref[kh * A_GROUP + gi] * LOG2E, sink)
        s_x = _dot_nt(qg, kx_ref[:, ks])
        s_p = jnp.where(mask_p, _dot_nt(qg, kp_ref[:, ks]), NEG)
        s_c = jnp.where(mask_c, _dot_nt(qg, kc_ref[:, ks]), NEG)
        s_n = jnp.where(mask_n, _dot_nt(qg, kn_ref[:, ks]), NEG)
        mx = jnp.maximum(jnp.maximum(s_x[:, :BLOCK], s_x[:, BLOCK:]), jnp.maximum(s_c[:, :BLOCK], s_c[:, BLOCK:]))
        mx = jnp.maximum(mx, jnp.maximum(s_p, s_n))
        mx = jnp.maximum(jnp.max(mx, axis=-1, keepdims=True), sink)
        e_x = jnp.exp2(s_x - mx)
        e_p = jnp.exp2(s_p - mx)
        e_c = jnp.exp2(s_c - mx)
        e_n = jnp.exp2(s_n - mx)
        tot = (e_x[:, :BLOCK] + e_x[:, BLOCK:]) + (e_c[:, :BLOCK] + e_c[:, BLOCK:]) + (e_p + e_n)
        den = jnp.exp2(sink - mx) + jnp.sum(tot, axis=-1, keepdims=True)
        o = (_dot(e_x.astype(BF16), vx_ref[:, ks]) + _dot(e_p.astype(BF16), vp_ref[:, ks])
             + _dot(e_c.astype(BF16), vc_ref[:, ks]) + _dot(e_n.astype(BF16), vn_ref[:, ks]))
        o = o / den
        for gi in range(A_GROUP):
            hd = kh * A_GROUP + gi
            o_ref[:, hd * A_HEAD_DIM:(hd + 1) * A_HEAD_DIM] = o[gi * ROW_TILE:(gi + 1) * ROW_TILE].astype(BF16)


def _a_attn(q, k, v, sinks, n_lat, n_ctx, with_ctx_queries):
    n_lat_tiles = n_lat // ROW_TILE
    n_tiles = n_lat_tiles + (1 if with_ctx_queries else 0)
    per = ROW_TILE // BLOCK
    last = (n_lat + n_ctx) // BLOCK - 1
    cur = lambda i: (i, 0)
    prev = lambda i: (jnp.maximum(i * per - 1, 0), 0)
    nxt = lambda i: (jnp.minimum((i + 1) * per, last), 0)
    ctx = lambda i: (n_lat_tiles, 0)
    kv_specs = [
        pl.BlockSpec((BLOCK, A_KV_W), prev),
        pl.BlockSpec((ROW_TILE, A_KV_W), cur),
        pl.BlockSpec((BLOCK, A_KV_W), nxt),
        pl.BlockSpec((n_ctx, A_KV_W), ctx),
    ]
    return pl.pallas_call(
        functools.partial(_a_attn_kernel, n_lat_tiles),
        grid=(n_tiles,),
        in_specs=[pl.BlockSpec(memory_space=pltpu.SMEM), pl.BlockSpec((ROW_TILE, A_Q_W), cur)] + kv_specs + kv_specs,
        out_specs=pl.BlockSpec((ROW_TILE, A_Q_W), cur),
        out_shape=jax.ShapeDtypeStruct((n_tiles * ROW_TILE, A_Q_W), BF16),
        compiler_params=_params("parallel"),
        name="a_attn",
    )(sinks, q, k, k, k, k, v, v, v, v)


def _b_proj_kernel(x_ref, mods_ref, g_ref, w_ref, xpre_ref, gate_ref):
    m = mods_ref[0]
    h = _norm_mod(x_ref[...], g_ref[0:1], m[0:1], m[1:2]).astype(BF16)
    xpre_ref[...] = _dot(h, w_ref[:, :B_WIDTH])
    gate_ref[...] = jax.nn.gelu(_dot(h, w_ref[:, B_WIDTH:]), approximate=True)


def _b_proj(xa, mods, g, w_in, n_lat_tiles):
    n_tiles = xa.shape[0] // ROW_TILE
    row = lambda i: (i, 0)
    return pl.pallas_call(
        _b_proj_kernel,
        grid=(n_tiles,),
        in_specs=[
            pl.BlockSpec((ROW_TILE, D_MODEL), row),
            _mods_spec(n_lat_tiles),
            _resident(g.shape),
            _resident(w_in.shape),
        ],
        out_specs=[pl.BlockSpec((ROW_TILE, B_WIDTH), row), pl.BlockSpec((ROW_TILE, B_WIDTH), row)],
        out_shape=[jax.ShapeDtypeStruct((xa.shape[0], B_WIDTH), F32)] * 2,
        compiler_params=_params("parallel"),
        name="b_proj",
    )(xa, mods, g, w_in)


def _softplus(y):
    return jnp.maximum(y, 0.0) + jnp.log1p(jnp.exp(-jnp.abs(y)))


def _expm1(x, u):
    one = u == 1.0
    k = (u - 1.0) * x / jnp.where(one, 1.0, jnp.log(u))
    return jnp.where(one, x, jnp.where(x < -1.0, u - 1.0, k))


def _b_tile_index(reverse, n_lat_tiles, i):
    lat = (n_lat_tiles - i) if reverse else (i - 1)
    return jnp.where(i == 0, n_lat_tiles, lat)


def _b_scan_kernel(reverse, n_lat_tiles, *refs):
    if reverse:
        (xc_ref, xp_ref, xn_ref, cw_ref, cb_ref, lam_ref, wa_ref, ba_ref, wx_ref, bx_ref,
         hf_ref, gate_ref, out_ref, carry_ref) = refs
    else:
        (xc_ref, xp_ref, xn_ref, cw_ref, cb_ref, lam_ref, wa_ref, ba_ref, wx_ref, bx_ref,
         out_ref, carry_ref) = refs
    i = pl.program_id(0)
    tile = _b_tile_index(reverse, n_lat_tiles, i)

    @pl.when(i == 0)
    def _():
        carry_ref[...] = jnp.zeros(carry_ref.shape, F32)

    prev_ok = jnp.logical_and(tile >= 1, tile <= n_lat_tiles - 1)
    next_ok = tile <= n_lat_tiles - 2
    u = xc_ref[...]
    n = u.shape[0]
    row = lax.broadcasted_iota(jnp.int32, (n, 1), 0)
    pm = jnp.where(prev_ok, xp_ref[SUBLANE - 1:SUBLANE, :], 0.0)
    n0 = jnp.where(next_ok, xn_ref[0:1, :], 0.0)
    n1 = jnp.where(next_ok, xn_ref[1:2, :], 0.0)
    um1 = jnp.where(row == 0, pm, pltpu.roll(u, 1, 0))
    up1 = jnp.where(row == n - 1, n0, pltpu.roll(u, n - 1, 0))
    up2 = jnp.where(row == n - 2, n0, jnp.where(row == n - 1, n1, pltpu.roll(u, n - 2, 0)))
    cw = cw_ref[...]
    xb = cb_ref[...] + (cw[0:1] * um1 + cw[1:2] * u + cw[2:3] * up1 + cw[3:4] * up2)

    xb16 = xb.astype(BF16)
    rs, gs = [], []
    for nblk in range(B_BLOCKS):
        sl = slice(nblk * B_BLOCK_W, (nblk + 1) * B_BLOCK_W)
        rs.append(_dot(xb16[:, sl], wa_ref[nblk]))
        gs.append(_dot(xb16[:, sl], wx_ref[nblk]))
    r = jax.nn.sigmoid(jnp.concatenate(rs, axis=1) + ba_ref[...])
    gi = jax.nn.sigmoid(jnp.concatenate(gs, axis=1) + bx_ref[...])
    log_a = (-B_LRU_C * r) * _softplus(-lam_ref[...])
    a = jnp.exp(log_a)
    b = jnp.sqrt(-_expm1(2.0 * log_a, a * a)) * (gi * xb)

    sub = row % SUBLANE
    s = 1
    while s < SUBLANE:
        if reverse:
            shift, valid = n - s, sub < SUBLANE - s
        else:
            shift, valid = s, sub >= s
        a_sh = pltpu.roll(a, shift, 0)
        b_sh = pltpu.roll(b, shift, 0)
        b = jnp.where(valid, a * b_sh + b, b)
        a = jnp.where(valid, a * a_sh, a)
        s *= 2
    carry = carry_ref[0:1, :]
    n_groups = n // SUBLANE
    hs = [None] * n_groups
    for k in range(n_groups):
        gidx = n_groups - 1 - k if reverse else k
        rows = slice(gidx * SUBLANE, (gidx + 1) * SUBLANE)
        hg = a[rows] * carry + b[rows]
        carry = hg[0:1, :] if reverse else hg[SUBLANE - 1:SUBLANE, :]
        hs[gidx] = hg
    carry_ref[0:1, :] = carry
    h = jnp.concatenate(hs, axis=0)
    if reverse:
        out_ref[...] = ((hf_ref[...] + h) * gate_ref[...]).astype(out_ref.dtype)
    else:
        out_ref[...] = h


def _b_scan(reverse, xpre, conv_w, conv_b, lam, w_a, b_a, w_x, b_x, n_lat_tiles, hf=None, gate=None):
    n_tiles = xpre.shape[0] // ROW_TILE
    per8 = ROW_TILE // SUBLANE
    last8 = xpre.shape[0] // SUBLANE - 1
    tile = functools.partial(_b_tile_index, reverse, n_lat_tiles)
    cur = lambda i: (tile(i), 0)
    prev8 = lambda i: (jnp.maximum(tile(i) * per8 - 1, 0), 0)
    next8 = lambda i: (jnp.minimum((tile(i) + 1) * per8, last8), 0)
    w = B_WIDTH
    in_specs = [
        pl.BlockSpec((ROW_TILE, w), cur),
        pl.BlockSpec((SUBLANE, w), prev8),
        pl.BlockSpec((SUBLANE, w), next8),
        _resident(conv_w.shape), _resident(conv_b.shape), _resident(lam.shape),
        _resident(w_a.shape), _resident(b_a.shape), _resident(w_x.shape), _resident(b_x.shape),
    ]
    args = [xpre, xpre, xpre, conv_w, conv_b, lam, w_a, b_a, w_x, b_x]
    if reverse:
        in_specs += [pl.BlockSpec((ROW_TILE, w), cur), pl.BlockSpec((ROW_TILE, w), cur)]
        args += [hf, gate]
    return pl.pallas_call(
        functools.partial(_b_scan_kernel, reverse, n_lat_tiles),
        grid=(n_tiles,),
        in_specs=in_specs,
        out_specs=pl.BlockSpec((ROW_TILE, w), cur),
        out_shape=jax.ShapeDtypeStruct(xpre.shape, BF16 if reverse else F32),
        scratch_shapes=[pltpu.VMEM((SUBLANE, w), F32)],
        compiler_params=_params("arbitrary"),
        name="b_scan_rev" if reverse else "b_scan_fwd",
    )(*args)


C_IN_PAD = C_Q_RANK + C_KV_RANK + LANE
C_QR_PAD = C_HEADS * LANE


def _c_proj_kernel(n_lat_tiles, x_ref, mods_ref, g_ref, win_ref, gq_ref, gkv_ref, wuq_ref, wuk_ref, wuvt_ref,
                   rc_ref, rs_ref, cc_ref, cs_ref, qn_ref, qr_ref, kn_ref, kr_ref, vt_ref):
    m = mods_ref[0]
    h = _norm_mod(x_ref[...], g_ref[0:1], m[0:1], m[1:2]).astype(BF16)
    z = _dot(h, win_ref[...])
    cq = _rms(z[:, :C_Q_RANK], gq_ref[...]).astype(BF16)
    ckv = _rms(z[:, C_Q_RANK:C_Q_RANK + C_KV_RANK], gkv_ref[...]).astype(BF16)
    cos, sin = _rope_tile(pl.program_id(0) >= n_lat_tiles, rc_ref, rs_ref, cc_ref, cs_ref)
    half = C_ROPE // 4
    kr_ref[...] = _rope(z[:, C_Q_RANK + C_KV_RANK:], cos, sin, half).astype(BF16)
    q = _dot(cq, wuq_ref[...]) * C_QSCALE
    qn_ref[...] = q[:, :C_HEADS * C_NOPE].astype(BF16)
    for hd in range(C_HEADS):
        lo = C_HEADS * C_NOPE + hd * LANE
        qr_ref[:, hd * LANE:(hd + 1) * LANE] = _rope(q[:, lo:lo + LANE], cos, sin, half).astype(BF16)
    kn_ref[...] = _dot(ckv, wuk_ref[...]).astype(BF16)
    vt_ref[...] = _dot_nt(wuvt_ref[...], ckv).astype(BF16)


def _c_proj(xa, mods, g, w_in, g_q, g_kv, w_uq, w_uk, w_uvt, rope, n_lat_tiles):
    n_rows = xa.shape[0]
    n_tiles = n_rows // ROW_TILE
    row = lambda i: (i, 0)
    hw = C_HEADS * C_NOPE
    return pl.pallas_call(
        functools.partial(_c_proj_kernel, n_lat_tiles),
        grid=(n_tiles,),
        in_specs=[
            pl.BlockSpec((ROW_TILE, D_MODEL), row),
            _mods_spec(n_lat_tiles),
            _resident(g.shape), _resident(w_in.shape), _resident(g_q.shape), _resident(g_kv.shape),
            _resident(w_uq.shape), _resident(w_uk.shape), _resident(w_uvt.shape),
        ] + _rope_specs(),
        out_specs=[
            pl.BlockSpec((ROW_TILE, hw), row),
            pl.BlockSpec((ROW_TILE, C_QR_PAD), row),
            pl.BlockSpec((ROW_TILE, hw), row),
            pl.BlockSpec((ROW_TILE, LANE), row),
            pl.BlockSpec((C_HEADS * C_VDIM, ROW_TILE), lambda i: (0, i)),
        ],
        out_shape=[
            jax.ShapeDtypeStruct((n_rows, hw), BF16),
            jax.ShapeDtypeStruct((n_rows, C_QR_PAD), BF16),
            jax.ShapeDtypeStruct((n_rows, hw), BF16),
            jax.ShapeDtypeStruct((n_rows, LANE), BF16),
            jax.ShapeDtypeStruct((C_HEADS * C_VDIM, n_rows), BF16),
        ],
        compiler_params=_params("parallel"),
        name="c_proj",
    )(xa, mods, g, w_in, g_q, g_kv, w_uq, w_uk, w_uvt, *rope)


def _c_scores(hd, qn_ref, qr_ref, kn_ref, kr_ref):
    hs = slice(hd * LANE, (hd + 1) * LANE)
    qcat = jnp.concatenate([qn_ref[:, hs], qr_ref[:, hs]], axis=1)
    kcat = jnp.concatenate([kn_ref[:, hs], kr_ref[...]], axis=1)
    return _dot_nt(kcat, qcat)


def _c_key_block_exact(qn_ref, qr_ref, kn_ref, kr_ref, vt_ref, m_ref, l_ref, acc_ref):
    for hd in range(C_HEADS):
        hs = slice(hd * LANE, (hd + 1) * LANE)
        st = _c_scores(hd, qn_ref, qr_ref, kn_ref, kr_ref)
        m_old = m_ref[hd:hd + 1, :]
        m_new = jnp.maximum(m_old, jnp.max(st, axis=0, keepdims=True))
        alpha = jnp.exp2(m_old - m_new)
        p = jnp.exp2(st - m_new)
        l_ref[hd:hd + 1, :] = alpha * l_ref[hd:hd + 1, :] + jnp.sum(p, axis=0, keepdims=True)
        acc_ref[hd] = alpha * acc_ref[hd] + _dot(vt_ref[hs, :], p.astype(BF16))
        m_ref[hd:hd + 1, :] = m_new


def _c_key_block_fixed_ref(qn_ref, qr_ref, kn_ref, kr_ref, vt_ref, m_ref, ps_ref, pv_ref):
    for hd in range(C_HEADS):
        hs = slice(hd * LANE, (hd + 1) * LANE)
        p = jnp.exp2(_c_scores(hd, qn_ref, qr_ref, kn_ref, kr_ref) - m_ref[hd:hd + 1, :])
        ps_ref[hd:hd + 1, :] = jnp.sum(p, axis=0, keepdims=True)
        pv_ref[hd] = _dot(vt_ref[hs, :], p.astype(BF16))


def _c_init(m_ref, l_ref, acc_ref):
    m_ref[...] = jnp.full(m_ref.shape, NEG, F32)
    l_ref[...] = jnp.zeros(l_ref.shape, F32)
    acc_ref[...] = jnp.zeros(acc_ref.shape, F32)


def _c_finish(l_ref, acc_ref, o_ref):
    for hd in range(C_HEADS):
        ot = acc_ref[hd] / l_ref[hd:hd + 1, :]
        o_ref[:, hd * C_VDIM:(hd + 1) * C_VDIM] = ot.T.astype(o_ref.dtype)


def _c_attn_lat_kernel(qn_ref, qr_ref, kn_ref, kr_ref, vt_ref, knx_ref, krx_ref, vtx_ref, o_ref,
                       m_ref, l_ref, acc_ref, ps_ref, pv_ref):
    j = pl.program_id(1)

    @pl.when(j == 0)
    def _():
        _c_init(m_ref, l_ref, acc_ref)
        _c_key_block_exact(qn_ref, qr_ref, knx_ref, krx_ref, vtx_ref, m_ref, l_ref, acc_ref)

    _c_key_block_fixed_ref(qn_ref, qr_ref, kn_ref, kr_ref, vt_ref, m_ref, ps_ref, pv_ref)
    safe = jnp.max(ps_ref[...]) < C_SAFE_SUM

    @pl.when(safe)
    def _():
        l_ref[...] = l_ref[...] + ps_ref[...]
        acc_ref[...] = acc_ref[...] + pv_ref[...]

    @pl.when(jnp.logical_not(safe))
    def _():
        _c_key_block_exact(qn_ref, qr_ref, kn_ref, kr_ref, vt_ref, m_ref, l_ref, acc_ref)

    @pl.when(j == pl.num_programs(1) - 1)
    def _():
        _c_finish(l_ref, acc_ref, o_ref)


def _c_attn_ctx_kernel(qn_ref, qr_ref, knx_ref, krx_ref, vtx_ref, o_ref, m_ref, l_ref, acc_ref):
    _c_init(m_ref, l_ref, acc_ref)
    _c_key_block_exact(qn_ref, qr_ref, knx_ref, krx_ref, vtx_ref, m_ref, l_ref, acc_ref)
    _c_finish(l_ref, acc_ref, o_ref)


def _c_scratch(tq, with_uncommitted):
    stats = [pltpu.VMEM((C_HEADS, tq), F32), pltpu.VMEM((C_HEADS, tq), F32), pltpu.VMEM((C_HEADS, C_VDIM, tq), F32)]
    return stats + (stats[1:] if with_uncommitted else [])


def _c_attn(qn, qr, kn, kr, vt, n_lat, n_ctx, with_ctx_queries):
    tq = min(512, n_lat)
    tk = min(2048, n_lat)
    hw = C_HEADS * C_NOPE
    cb = n_lat // n_ctx
    qrow = lambda i, j: (i, 0)
    krow = lambda i, j: (j, 0)
    o = pl.pallas_call(
        _c_attn_lat_kernel,
        grid=(n_lat // tq, n_lat // tk),
        in_specs=[
            pl.BlockSpec((tq, hw), qrow),
            pl.BlockSpec((tq, C_QR_PAD), qrow),
            pl.BlockSpec((tk, hw), krow),
            pl.BlockSpec((tk, LANE), krow),
            pl.BlockSpec((hw, tk), lambda i, j: (0, j)),
            pl.BlockSpec((n_ctx, hw), lambda i, j: (cb, 0)),
            pl.BlockSpec((n_ctx, LANE), lambda i, j: (cb, 0)),
            pl.BlockSpec((hw, n_ctx), lambda i, j: (0, cb)),
        ],
        out_specs=pl.BlockSpec((tq, hw), qrow),
        out_shape=jax.ShapeDtypeStruct((n_lat, hw), BF16),
        scratch_shapes=_c_scratch(tq, True),
        compiler_params=_params("parallel", "arbitrary"),
        name="c_attn_lat",
    )(qn, qr, kn, kr, vt, kn, kr, vt)
    if not with_ctx_queries:
        return o, o, 0
    o_ctx = pl.pallas_call(
        _c_attn_ctx_kernel,
        grid=(1,),
        in_specs=[
            pl.BlockSpec((n_ctx, hw), lambda i: (cb, 0)),
            pl.BlockSpec((n_ctx, C_QR_PAD), lambda i: (cb, 0)),
            pl.BlockSpec((n_ctx, hw), lambda i: (cb, 0)),
            pl.BlockSpec((n_ctx, LANE), lambda i: (cb, 0)),
            pl.BlockSpec((hw, n_ctx), lambda i: (0, cb)),
        ],
        out_specs=pl.BlockSpec((n_ctx, hw), lambda i: (0, 0)),
        out_shape=jax.ShapeDtypeStruct((n_ctx, hw), BF16),
        scratch_shapes=_c_scratch(n_ctx, False),
        compiler_params=_params("arbitrary"),
        name="c_attn_ctx",
    )(qn, qr, kn, kr, vt)
    return o, o_ctx, 0


def _rope_tables(n_lat, d_rot):
    n_rows = n_lat // GRID_W
    d_axis = d_rot // 2
    quarter = d_rot // 4
    inv = 1.0 / (ROPE_BASE ** (jnp.arange(0, d_axis, 2, dtype=F32) / d_axis))
    ar = jnp.arange(n_rows, dtype=F32)[:, None] * inv
    ac = jnp.arange(GRID_W, dtype=F32)[:, None] * inv

    def rows(a, b):
        t = jnp.pad(jnp.concatenate([a, b], axis=1), ((0, 0), (0, LANE - 2 * quarter)))
        t = t.reshape(n_rows // ROWS_PER_TILE, ROWS_PER_TILE, LANE)
        return jnp.pad(t, ((0, 1), (0, SUBLANE - ROWS_PER_TILE), (0, 0)))

    def cols(a, b):
        t = jnp.pad(jnp.concatenate([a, b], axis=1), ((0, 0), (2 * quarter, LANE - 4 * quarter)))
        return jnp.tile(t, (ROWS_PER_TILE, 1))

    cr, sr, cc, sc = jnp.cos(ar), jnp.sin(ar), jnp.cos(ac), jnp.sin(ac)
    return rows(cr, cr), rows(-sr, sr), cols(cc, cc), cols(-sc, sc)


def _rope_specs():
    row = pl.BlockSpec((1, SUBLANE, LANE), lambda i: (i, 0, 0))
    return [row, row, _resident((ROW_TILE, LANE)), _resident((ROW_TILE, LANE))]


def _rope_tile(is_ctx, row_cos_ref, row_sin_ref, col_cos_ref, col_sin_ref):
    def spread(ref):
        t = ref[0]
        return jnp.concatenate([jnp.broadcast_to(t[r:r + 1, :], (GRID_W, LANE)) for r in range(ROWS_PER_TILE)], axis=0)
    cos = jnp.where(is_ctx, 1.0, spread(row_cos_ref) + col_cos_ref[...])
    sin = jnp.where(is_ctx, 0.0, spread(row_sin_ref) + col_sin_ref[...])
    return cos, sin


def _pad_cols(w, width):
    return jnp.pad(w, ((0, 0), (0, width - w.shape[1])))


def kernel(x, c, ctx, c_ctx, ada_w, ada_b, norm_g, ffn_w_gu, ffn_w_down, a_w_qkv, a_sinks, a_w_o, b_w_in, b_conv_w, b_conv_b, b_lam, b_w_a, b_b_a, b_w_x, b_b_x, b_w_out, c_w_in, c_g_q, c_g_kv, c_w_uq, c_w_ukv, c_w_out):
    assert x.shape[0] == 1 and ctx.shape[0] == 1 and x.shape[2] == D_MODEL
    n_lat, n_ctx = x.shape[1], ctx.shape[1]
    assert n_ctx == ROW_TILE and n_lat % ROW_TILE == 0 and n_lat % GRID_W == 0
    n_lat_tiles = n_lat // ROW_TILE
    n_all_tiles = n_lat_tiles + 1

    x_stream = (x[0], ctx[0], 0)
    s_rows = jnp.zeros((SUBLANE, D_MODEL), F32).at[0].set(c[0]).at[1].set(c_ctx)
    mods_all = _ada(s_rows, ada_w, ada_b)[:, :2].reshape(DEPTH, 2, N_MOD, D_MODEL)
    w_gu_all = ffn_w_gu.astype(BF16)
    w_down_all = ffn_w_down.astype(BF16)
    rope_a = _rope_tables(n_lat, A_HEAD_DIM)
    rope_c = _rope_tables(n_lat, C_ROPE)

    for i in range(DEPTH):
        last = i == DEPTH - 1
        kind, j = i % N_MIXERS, i // N_MIXERS
        mods = mods_all[i]
        g = norm_g[i]
        if kind == 0:
            q, k, v = _a_proj(x_stream, mods, g, a_w_qkv[j].astype(BF16), rope_a, n_lat_tiles)
            o = _a_attn(q, k, v, a_sinks[j], n_lat, n_ctx, not last)
            o_stream = _stream(o, n_lat_tiles) if not last else (o, o, 0)
            w_o = a_w_o[j]
        elif kind == 1:
            xa = x_stream[0]
            xpre, gate = _b_proj(xa, mods, g, b_w_in[j].astype(BF16), n_lat_tiles)
            scan_args = lambda d: (b_conv_w[j], b_conv_b[j][None], b_lam[j, d][None], b_w_a[j, d].astype(BF16),
                                   b_b_a[j, d][None], b_w_x[j, d].astype(BF16), b_b_x[j, d][None], n_lat_tiles)
            hf = _b_scan(False, xpre, *scan_args(0))
            o_stream = _stream(_b_scan(True, xpre, *scan_args(1), hf=hf, gate=gate), n_lat_tiles)
            w_o = b_w_out[j]
        else:
            xa = x_stream[0]
            w_in = c_w_in[j]
            w_in_p = _pad_cols(w_in, C_IN_PAD).astype(BF16)
            w_uq3 = c_w_uq[j].reshape(C_Q_RANK, C_HEADS, C_NOPE + C_ROPE)
            w_uq_p = jnp.concatenate(
                [w_uq3[..., :C_NOPE].reshape(C_Q_RANK, C_HEADS * C_NOPE),
                 jnp.pad(w_uq3[..., C_NOPE:], ((0, 0), (0, 0), (0, LANE - C_ROPE))).reshape(C_Q_RANK, C_QR_PAD)],
                axis=1).astype(BF16)
            w_ukv3 = c_w_ukv[j].reshape(C_KV_RANK, C_HEADS, C_NOPE + C_VDIM)
            w_uk = w_ukv3[..., :C_NOPE].reshape(C_KV_RANK, C_HEADS * C_NOPE).astype(BF16)
            w_uvt = w_ukv3[..., C_NOPE:].reshape(C_KV_RANK, C_HEADS * C_VDIM).T.astype(BF16)
            qn, qr, kn, kr, vt = _c_proj(xa, mods, g, w_in_p, c_g_q[j][None], c_g_kv[j][None], w_uq_p, w_uk, w_uvt,
                                         rope_c, n_lat_tiles)
            o_stream = _c_attn(qn, qr, kn, kr, vt, n_lat, n_ctx, not last)
            w_o = c_w_out[j]
        xa = _post(o_stream, x_stream, mods, g, w_o.astype(BF16), w_gu_all, w_down_all, i, n_lat_tiles,
                   n_lat_tiles if last else n_all_tiles)
        x_stream = _stream(xa, n_lat_tiles)
    return xa[None, :n_lat]
```

```python
import functools
import math

import jax
import jax.numpy as jnp
from jax import lax
from jax.experimental import pallas as pl
from jax.experimental.pallas import tpu as pltpu

D_MODEL = 1024
DEPTH = 4
GRID_W = 64
N_MIXERS = 3
BLOCK = 128
ROPE_BASE = 10000.0
RMS_EPS = 1e-6
N_MOD = 6

A_HEADS = 8
A_KV_HEADS = 2
A_GROUP = A_HEADS // A_KV_HEADS
A_HEAD_DIM = D_MODEL // A_HEADS
A_WINDOW = 128
A_SCALE = A_HEAD_DIM ** -0.5
LOG2E = math.log2(math.e)
A_QSCALE = A_SCALE * LOG2E
A_Q_W = A_HEADS * A_HEAD_DIM
A_KV_W = A_KV_HEADS * A_HEAD_DIM
A_QKV = A_Q_W + 2 * A_KV_W

B_WIDTH = D_MODEL
B_BLOCKS = 4
B_BLOCK_W = B_WIDTH // B_BLOCKS
B_CONV_W = 4
B_LRU_C = 8.0

C_HEADS = 8
C_NOPE = 128
C_ROPE = 64
C_VDIM = 128
C_Q_RANK = D_MODEL // 2
C_KV_RANK = D_MODEL // 4
C_SCALE = (C_NOPE + C_ROPE) ** -0.5
C_QSCALE = C_SCALE * LOG2E

FFN_HIDDEN = ((8 * D_MODEL + 3 * 256 - 1) // (3 * 256)) * 256

LANE = 128
SUBLANE = 8
ROW_TILE = 256
ROWS_PER_TILE = ROW_TILE // GRID_W
FFN_CHUNK = 256
C_SAFE_SUM = 2.0 ** 60
VMEM_LIMIT = 56 * 1024 * 1024
NEG = -1e30

BF16 = jnp.bfloat16
F32 = jnp.float32


def _dot(a, b):
    return jnp.dot(a, b, preferred_element_type=F32)


def _dot_nt(a, b):
    return lax.dot_general(a, b, (((1,), (1,)), ((), ())), preferred_element_type=F32)


def _resident(shape):
    nd = len(shape)
    return pl.BlockSpec(shape, lambda *_: (0,) * nd, pipeline_mode=pl.Buffered(1))


def _resident_layer(stack_shape, layer):
    nd = len(stack_shape) - 1
    return pl.BlockSpec((None,) + tuple(stack_shape[1:]), lambda *_: (layer,) + (0,) * nd, pipeline_mode=pl.Buffered(1))


def _params(*sem):
    return pltpu.CompilerParams(dimension_semantics=sem, vmem_limit_bytes=VMEM_LIMIT)


def _rms(x, g):
    ms = jnp.mean(x * x, axis=-1, keepdims=True)
    return x * lax.rsqrt(ms + RMS_EPS) * g


def _norm_mod(x, g, shift, scale):
    return _rms(x, g) * (1.0 + scale) + shift


def _rope(z, cos, sin, half):
    lane = lax.broadcasted_iota(jnp.int32, z.shape, 1)
    first = (lane % (2 * half)) < half
    partner = jnp.where(first, pltpu.roll(z, LANE - half, 1), pltpu.roll(z, half, 1))
    return z * cos + partner * sin


def _ada_kernel(s_ref, w_ref, b_ref, o_ref):
    s = s_ref[...]
    s = s * jax.nn.sigmoid(s)
    o_ref[0] = _dot(s.astype(BF16), w_ref[0].astype(BF16)) + b_ref[0]


def _ada(s_rows, ada_w, ada_b):
    tn = 1536
    depth, d, n = ada_w.shape
    return pl.pallas_call(
        _ada_kernel,
        grid=(depth, n // tn),
        in_specs=[
            pl.BlockSpec((SUBLANE, d), lambda l, j: (0, 0)),
            pl.BlockSpec((1, d, tn), lambda l, j: (l, 0, j)),
            pl.BlockSpec((1, 1, tn), lambda l, j: (l, 0, j)),
        ],
        out_specs=pl.BlockSpec((1, SUBLANE, tn), lambda l, j: (l, 0, j)),
        out_shape=jax.ShapeDtypeStruct((depth, SUBLANE, n), F32),
        compiler_params=_params("parallel", "parallel"),
        name="ada",
    )(s_rows, ada_w, ada_b.reshape(depth, 1, n))


def _mods_spec(n_lat_tiles):
    return pl.BlockSpec((1, N_MOD, D_MODEL), lambda i: (jnp.where(i >= n_lat_tiles, 1, 0), 0, 0))


def _stream(arr, n_lat_tiles):
    return arr, arr, n_lat_tiles


def _stream_specs(width, n_lat_tiles, ctx_tile):
    return [pl.BlockSpec((ROW_TILE, width), lambda i: (jnp.minimum(i, n_lat_tiles - 1), 0)),
            pl.BlockSpec((ROW_TILE, width), lambda i: (ctx_tile, 0))]


def _stream_tile(n_lat_tiles, lat_ref, ctx_ref):
    return jnp.where(pl.program_id(0) >= n_lat_tiles, ctx_ref[...], lat_ref[...])


def _post_kernel(n_lat_tiles, ol_ref, oc_ref, xl_ref, xc_ref, mods_ref, g_ref, wo_ref, wgu_ref, wd_ref, out_ref, act_ref):
    m = mods_ref[0]
    g = g_ref[...]
    y = _dot(_stream_tile(n_lat_tiles, ol_ref, oc_ref), wo_ref[...])
    x1 = _stream_tile(n_lat_tiles, xl_ref, xc_ref) + m[2:3] * _rms(y, g[1:2])
    h = _norm_mod(x1, g[2:3], m[3:4], m[4:5]).astype(BF16)
    for c in range(FFN_HIDDEN // FFN_CHUNK):
        lo = c * FFN_CHUNK
        zg = _dot(h, wgu_ref[:, lo:lo + FFN_CHUNK])
        zu = _dot(h, wgu_ref[:, FFN_HIDDEN + lo:FFN_HIDDEN + lo + FFN_CHUNK])
        act_ref[:, lo:lo + FFN_CHUNK] = (zg * jax.nn.sigmoid(zg) * zu).astype(BF16)
    f = _dot(act_ref[...], wd_ref[...])
    out_ref[...] = x1 + m[5:6] * _rms(f, g[3:4])


def _post(o_stream, x_stream, mods, g, w_o, w_gu, w_down, layer, n_lat_tiles, n_tiles):
    d = D_MODEL
    row = lambda i: (i, 0)
    o_lat, o_ctx, o_ctx_tile = o_stream
    x_lat, x_ctx, x_ctx_tile = x_stream
    return pl.pallas_call(
        functools.partial(_post_kernel, n_lat_tiles),
        grid=(n_tiles,),
        in_specs=_stream_specs(w_o.shape[0], n_lat_tiles, o_ctx_tile) + _stream_specs(d, n_lat_tiles, x_ctx_tile) + [
            _mods_spec(n_lat_tiles),
            _resident(g.shape),
            _resident(w_o.shape),
            _resident_layer(w_gu.shape, layer),
            _resident_layer(w_down.shape, layer),
        ],
        out_specs=pl.BlockSpec((ROW_TILE, d), row),
        out_shape=jax.ShapeDtypeStruct((n_tiles * ROW_TILE, d), F32),
        scratch_shapes=[pltpu.VMEM((ROW_TILE, FFN_HIDDEN), BF16)],
        compiler_params=_params("parallel"),
        name="post_ffn",
    )(o_lat, o_ctx, x_lat, x_ctx, mods, g, w_o, w_gu, w_down)


def _a_proj_kernel(n_lat_tiles, xl_ref, xc_ref, mods_ref, g_ref, w_ref, rc_ref, rs_ref, cc_ref, cs_ref, q_ref, k_ref, v_ref):
    m = mods_ref[0]
    h = _norm_mod(_stream_tile(n_lat_tiles, xl_ref, xc_ref), g_ref[0:1], m[0:1], m[1:2]).astype(BF16)
    z = _dot(h, w_ref[...])
    cos, sin = _rope_tile(pl.program_id(0) >= n_lat_tiles, rc_ref, rs_ref, cc_ref, cs_ref)
    half = A_HEAD_DIM // 4
    for hd in range(A_HEADS):
        lo = hd * A_HEAD_DIM
        q_ref[:, lo:lo + A_HEAD_DIM] = (_rope(z[:, lo:lo + A_HEAD_DIM], cos, sin, half) * A_QSCALE).astype(BF16)
    for hd in range(A_KV_HEADS):
        lo = hd * A_HEAD_DIM
        k_ref[:, lo:lo + A_HEAD_DIM] = _rope(z[:, A_Q_W + lo:A_Q_W + lo + A_HEAD_DIM], cos, sin, half).astype(BF16)
    v_ref[...] = z[:, A_Q_W + A_KV_W:].astype(BF16)


def _a_proj(x_stream, mods, g, w_qkv, rope, n_lat_tiles):
    n_tiles = n_lat_tiles + 1
    n_rows = n_tiles * ROW_TILE
    row = lambda i: (i, 0)
    x_lat, x_ctx, x_ctx_tile = x_stream
    return pl.pallas_call(
        functools.partial(_a_proj_kernel, n_lat_tiles),
        grid=(n_tiles,),
        in_specs=_stream_specs(D_MODEL, n_lat_tiles, x_ctx_tile) + [
            _mods_spec(n_lat_tiles),
            _resident(g.shape),
            _resident(w_qkv.shape),
        ] + _rope_specs(),
        out_specs=[
            pl.BlockSpec((ROW_TILE, A_Q_W), row),
            pl.BlockSpec((ROW_TILE, A_KV_W), row),
            pl.BlockSpec((ROW_TILE, A_KV_W), row),
        ],
        out_shape=[
            jax.ShapeDtypeStruct((n_rows, A_Q_W), BF16),
            jax.ShapeDtypeStruct((n_rows, A_KV_W), BF16),
            jax.ShapeDtypeStruct((n_rows, A_KV_W), BF16),
        ],
        compiler_params=_params("parallel"),
        name="a_proj",
    )(x_lat, x_ctx, mods, g, w_qkv, *rope)


def _a_attn_kernel(n_lat_tiles, sink_ref, q_ref, kp_ref, kc_ref, kn_ref, kx_ref, vp_ref, vc_ref, vn_ref, vx_ref, o_ref):
    i = pl.program_id(0)
    is_lat = i < n_lat_tiles
    prev_ok = jnp.logical_and(is_lat, i >= 1)
    next_ok = i < n_lat_tiles - 1
    rows = A_GROUP * ROW_TILE
    r = lax.broadcasted_iota(jnp.int32, (rows, BLOCK), 0) % ROW_TILE
    c = lax.broadcasted_iota(jnp.int32, (rows, BLOCK), 1)
    r2 = lax.broadcasted_iota(jnp.int32, (rows, ROW_TILE), 0) % ROW_TILE
    c2 = lax.broadcasted_iota(jnp.int32, (rows, ROW_TILE), 1)
    mask_p = jnp.logical_and(c >= r, prev_ok)
    mask_n = jnp.logical_and(c <= r - A_WINDOW, next_ok)
    mask_c = jnp.logical_and(jnp.abs(r2 - c2) <= A_WINDOW, is_lat)
    grp = lax.broadcasted_iota(jnp.int32, (rows, 1), 0) // ROW_TILE
    for kh in range(A_KV_HEADS):
        ks = slice(kh * A_HEAD_DIM, (kh + 1) * A_HEAD_DIM)
        qg = jnp.concatenate(
            [q_ref[:, (kh * A_GROUP + gi) * A_HEAD_DIM:(kh * A_GROUP + gi + 1) * A_HEAD_DIM] for gi in range(A_GROUP)],
            axis=0)
        sink = jnp.zeros((rows, 1), F32)
        for gi in range(A_GROUP):
            sink = jnp.where(grp == gi, sink_ref[kh * A_GROUP + gi] * LOG2E, sink)
        s_x = _dot_nt(qg, kx_ref[:, ks])
        s_p = jnp.where(mask_p, _dot_nt(qg, kp_ref[:, ks]), NEG)
        s_c = jnp.where(mask_c, _dot_nt(qg, kc_ref[:, ks]), NEG)
        s_n = jnp.where(mask_n, _dot_nt(qg, kn_ref[:, ks]), NEG)
        mx = jnp.maximum(jnp.maximum(s_x[:, :BLOCK], s_x[:, BLOCK:]), jnp.maximum(s_c[:, :BLOCK], s_c[:, BLOCK:]))
        mx = jnp.maximum(mx, jnp.maximum(s_p, s_n))
        mx = jnp.maximum(jnp.max(mx, axis=-1, keepdims=True), sink)
        e_x = jnp.exp2(s_x - mx)
        e_p = jnp.exp2(s_p - mx)
        e_c = jnp.exp2(s_c - mx)
        e_n = jnp.exp2(s_n - mx)
        tot = (e_x[:, :BLOCK] + e_x[:, BLOCK:]) + (e_c[:, :BLOCK] + e_c[:, BLOCK:]) + (e_p + e_n)
        den = jnp.exp2(sink - mx) + jnp.sum(tot, axis=-1, keepdims=True)
        o = (_dot(e_x.astype(BF16), vx_ref[:, ks]) + _dot(e_p.astype(BF16), vp_ref[:, ks])
             + _dot(e_c.astype(BF16), vc_ref[:, ks]) + _dot(e_n.astype(BF16), vn_ref[:, ks]))
        o = o / den
        for gi in range(A_GROUP):
            hd = kh * A_GROUP + gi
            o_ref[:, hd * A_HEAD_DIM:(hd + 1) * A_HEAD_DIM] = o[gi * ROW_TILE:(gi + 1) * ROW_TILE].astype(BF16)


def _a_attn(q, k, v, sinks, n_lat, n_ctx, with_ctx_queries):
    n_lat_tiles = n_lat // ROW_TILE
    n_tiles = n_lat_tiles + (1 if with_ctx_queries else 0)
    per = ROW_TILE // BLOCK
    last = (n_lat + n_ctx) // BLOCK - 1
    cur = lambda i: (i, 0)
    prev = lambda i: (jnp.maximum(i * per - 1, 0), 0)
    nxt = lambda i: (jnp.minimum((i + 1) * per, last), 0)
    ctx = lambda i: (n_lat_tiles, 0)
    kv_specs = [
        pl.BlockSpec((BLOCK, A_KV_W), prev),
        pl.BlockSpec((ROW_TILE, A_KV_W), cur),
        pl.BlockSpec((BLOCK, A_KV_W), nxt),
        pl.BlockSpec((n_ctx, A_KV_W), ctx),
    ]
    return pl.pallas_call(
        functools.partial(_a_attn_kernel, n_lat_tiles),
        grid=(n_tiles,),
        in_specs=[pl.BlockSpec(memory_space=pltpu.SMEM), pl.BlockSpec((ROW_TILE, A_Q_W), cur)] + kv_specs + kv_specs,
        out_specs=pl.BlockSpec((ROW_TILE, A_Q_W), cur),
        out_shape=jax.ShapeDtypeStruct((n_tiles * ROW_TILE, A_Q_W), BF16),
        compiler_params=_params("parallel"),
        name="a_attn",
    )(sinks, q, k, k, k, k, v, v, v, v)


def _b_proj_kernel(x_ref, mods_ref, g_ref, w_ref, xpre_ref, gate_ref):
    m = mods_ref[0]
    h = _norm_mod(x_ref[...], g_ref[0:1], m[0:1], m[1:2]).astype(BF16)
    xpre_ref[...] = _dot(h, w_ref[:, :B_WIDTH])
    gate_ref[...] = jax.nn.gelu(_dot(h, w_ref[:, B_WIDTH:]), approximate=True)


def _b_proj(xa, mods, g, w_in, n_lat_tiles):
    n_tiles = xa.shape[0] // ROW_TILE
    row = lambda i: (i, 0)
    return pl.pallas_call(
        _b_proj_kernel,
        grid=(n_tiles,),
        in_specs=[
            pl.BlockSpec((ROW_TILE, D_MODEL), row),
            _mods_spec(n_lat_tiles),
            _resident(g.shape),
            _resident(w_in.shape),
        ],
        out_specs=[pl.BlockSpec((ROW_TILE, B_WIDTH), row), pl.BlockSpec((ROW_TILE, B_WIDTH), row)],
        out_shape=[jax.ShapeDtypeStruct((xa.shape[0], B_WIDTH), F32)] * 2,
        compiler_params=_params("parallel"),
        name="b_proj",
    )(xa, mods, g, w_in)


def _softplus(y):
    return jnp.maximum(y, 0.0) + jnp.log1p(jnp.exp(-jnp.abs(y)))


def _b_tile_index(reverse, n_lat_tiles, i):
    lat = (n_lat_tiles - i) if reverse else (i - 1)
    return jnp.where(i == 0, n_lat_tiles, lat)


def _b_scan_kernel(reverse, n_lat_tiles, *refs):
    if reverse:
        (xc_ref, xp_ref, xn_ref, cw_ref, cb_ref, lam_ref, wa_ref, ba_ref, wx_ref, bx_ref,
         hf_ref, gate_ref, out_ref, carry_ref) = refs
    else:
        (xc_ref, xp_ref, xn_ref, cw_ref, cb_ref, lam_ref, wa_ref, ba_ref, wx_ref, bx_ref,
         out_ref, carry_ref) = refs
    i = pl.program_id(0)
    tile = _b_tile_index(reverse, n_lat_tiles, i)

    @pl.when(i == 0)
    def _():
        carry_ref[...] = jnp.zeros(carry_ref.shape, F32)

    prev_ok = jnp.logical_and(tile >= 1, tile <= n_lat_tiles - 1)
    next_ok = tile <= n_lat_tiles - 2
    u = xc_ref[...]
    n = u.shape[0]
    row = lax.broadcasted_iota(jnp.int32, (n, 1), 0)
    pm = jnp.where(prev_ok, xp_ref[SUBLANE - 1:SUBLANE, :], 0.0)
    n0 = jnp.where(next_ok, xn_ref[0:1, :], 0.0)
    n1 = jnp.where(next_ok, xn_ref[1:2, :], 0.0)
    um1 = jnp.where(row == 0, pm, pltpu.roll(u, 1, 0))
    up1 = jnp.where(row == n - 1, n0, pltpu.roll(u, n - 1, 0))
    up2 = jnp.where(row == n - 2, n0, jnp.where(row == n - 1, n1, pltpu.roll(u, n - 2, 0)))
    cw = cw_ref[...]
    xb = cb_ref[...] + (cw[0:1] * um1 + cw[1:2] * u + cw[2:3] * up1 + cw[3:4] * up2)

    xb16 = xb.astype(BF16)
    rs, gs = [], []
    for nblk in range(B_BLOCKS):
        sl = slice(nblk * B_BLOCK_W, (nblk + 1) * B_BLOCK_W)
        rs.append(_dot(xb16[:, sl], wa_ref[nblk]))
        gs.append(_dot(xb16[:, sl], wx_ref[nblk]))
    r = jax.nn.sigmoid(jnp.concatenate(rs, axis=1) + ba_ref[...])
    gi = jax.nn.sigmoid(jnp.concatenate(gs, axis=1) + bx_ref[...])
    log_a = (-B_LRU_C * r) * _softplus(-lam_ref[...])
    a = jnp.exp(log_a)
    b = jnp.sqrt(1.0 - a * a) * (gi * xb)

    sub = row % SUBLANE
    s = 1
    while s < SUBLANE:
        if reverse:
            shift, valid = n - s, sub < SUBLANE - s
        else:
            shift, valid = s, sub >= s
        a_sh = pltpu.roll(a, shift, 0)
        b_sh = pltpu.roll(b, shift, 0)
        b = jnp.where(valid, a * b_sh + b, b)
        a = jnp.where(valid, a * a_sh, a)
        s *= 2
    carry = carry_ref[0:1, :]
    n_groups = n // SUBLANE
    hs = [None] * n_groups
    for k in range(n_groups):
        gidx = n_groups - 1 - k if reverse else k
        rows = slice(gidx * SUBLANE, (gidx + 1) * SUBLANE)
        hg = a[rows] * carry + b[rows]
        carry = hg[0:1, :] if reverse else hg[SUBLANE - 1:SUBLANE, :]
        hs[gidx] = hg
    carry_ref[0:1, :] = carry
    h = jnp.concatenate(hs, axis=0)
    if reverse:
        out_ref[...] = ((hf_ref[...] + h) * gate_ref[...]).astype(out_ref.dtype)
    else:
        out_ref[...] = h


def _b_scan(reverse, xpre, conv_w, conv_b, lam, w_a, b_a, w_x, b_x, n_lat_tiles, hf=None, gate=None):
    n_tiles = xpre.shape[0] // ROW_TILE
    per8 = ROW_TILE // SUBLANE
    last8 = xpre.shape[0] // SUBLANE - 1
    tile = functools.partial(_b_tile_index, reverse, n_lat_tiles)
    cur = lambda i: (tile(i), 0)
    prev8 = lambda i: (jnp.maximum(tile(i) * per8 - 1, 0), 0)
    next8 = lambda i: (jnp.minimum((tile(i) + 1) * per8, last8), 0)
    w = B_WIDTH
    in_specs = [
        pl.BlockSpec((ROW_TILE, w), cur),
        pl.BlockSpec((SUBLANE, w), prev8),
        pl.BlockSpec((SUBLANE, w), next8),
        _resident(conv_w.shape), _resident(conv_b.shape), _resident(lam.shape),
        _resident(w_a.shape), _resident(b_a.shape), _resident(w_x.shape), _resident(b_x.shape),
    ]
    args = [xpre, xpre, xpre, conv_w, conv_b, lam, w_a, b_a, w_x, b_x]
    if reverse:
        in_specs += [pl.BlockSpec((ROW_TILE, w), cur), pl.BlockSpec((ROW_TILE, w), cur)]
        args += [hf, gate]
    return pl.pallas_call(
        functools.partial(_b_scan_kernel, reverse, n_lat_tiles),
        grid=(n_tiles,),
        in_specs=in_specs,
        out_specs=pl.BlockSpec((ROW_TILE, w), cur),
        out_shape=jax.ShapeDtypeStruct(xpre.shape, BF16 if reverse else F32),
        scratch_shapes=[pltpu.VMEM((SUBLANE, w), F32)],
        compiler_params=_params("arbitrary"),
        name="b_scan_rev" if reverse else "b_scan_fwd",
    )(*args)


C_IN_PAD = C_Q_RANK + C_KV_RANK + LANE
C_QR_PAD = C_HEADS * LANE


def _c_proj_kernel(n_lat_tiles, x_ref, mods_ref, g_ref, win_ref, gq_ref, gkv_ref, wuq_ref, wuk_ref, wuvt_ref,
                   rc_ref, rs_ref, cc_ref, cs_ref, qn_ref, qr_ref, kn_ref, kr_ref, vt_ref):
    m = mods_ref[0]
    h = _norm_mod(x_ref[...], g_ref[0:1], m[0:1], m[1:2]).astype(BF16)
    z = _dot(h, win_ref[...])
    cq = _rms(z[:, :C_Q_RANK], gq_ref[...]).astype(BF16)
    ckv = _rms(z[:, C_Q_RANK:C_Q_RANK + C_KV_RANK], gkv_ref[...]).astype(BF16)
    cos, sin = _rope_tile(pl.program_id(0) >= n_lat_tiles, rc_ref, rs_ref, cc_ref, cs_ref)
    half = C_ROPE // 4
    kr_ref[...] = _rope(z[:, C_Q_RANK + C_KV_RANK:], cos, sin, half).astype(BF16)
    q = _dot(cq, wuq_ref[...]) * C_QSCALE
    qn_ref[...] = q[:, :C_HEADS * C_NOPE].astype(BF16)
    for hd in range(C_HEADS):
        lo = C_HEADS * C_NOPE + hd * LANE
        qr_ref[:, hd * LANE:(hd + 1) * LANE] = _rope(q[:, lo:lo + LANE], cos, sin, half).astype(BF16)
    kn_ref[...] = _dot(ckv, wuk_ref[...]).astype(BF16)
    vt_ref[...] = _dot_nt(wuvt_ref[...], ckv).astype(BF16)


def _c_proj(xa, mods, g, w_in, g_q, g_kv, w_uq, w_uk, w_uvt, rope, n_lat_tiles):
    n_rows = xa.shape[0]
    n_tiles = n_rows // ROW_TILE
    row = lambda i: (i, 0)
    hw = C_HEADS * C_NOPE
    return pl.pallas_call(
        functools.partial(_c_proj_kernel, n_lat_tiles),
        grid=(n_tiles,),
        in_specs=[
            pl.BlockSpec((ROW_TILE, D_MODEL), row),
            _mods_spec(n_lat_tiles),
            _resident(g.shape), _resident(w_in.shape), _resident(g_q.shape), _resident(g_kv.shape),
            _resident(w_uq.shape), _resident(w_uk.shape), _resident(w_uvt.shape),
        ] + _rope_specs(),
        out_specs=[
            pl.BlockSpec((ROW_TILE, hw), row),
            pl.BlockSpec((ROW_TILE, C_QR_PAD), row),
            pl.BlockSpec((ROW_TILE, hw), row),
            pl.BlockSpec((ROW_TILE, LANE), row),
            pl.BlockSpec((C_HEADS * C_VDIM, ROW_TILE), lambda i: (0, i)),
        ],
        out_shape=[
            jax.ShapeDtypeStruct((n_rows, hw), BF16),
            jax.ShapeDtypeStruct((n_rows, C_QR_PAD), BF16),
            jax.ShapeDtypeStruct((n_rows, hw), BF16),
            jax.ShapeDtypeStruct((n_rows, LANE), BF16),
            jax.ShapeDtypeStruct((C_HEADS * C_VDIM, n_rows), BF16),
        ],
        compiler_params=_params("parallel"),
        name="c_proj",
    )(xa, mods, g, w_in, g_q, g_kv, w_uq, w_uk, w_uvt, *rope)


def _c_scores(hd, qn_ref, qr_ref, kn_ref, kr_ref):
    hs = slice(hd * LANE, (hd + 1) * LANE)
    qcat = jnp.concatenate([qn_ref[:, hs], qr_ref[:, hs]], axis=1)
    kcat = jnp.concatenate([kn_ref[:, hs], kr_ref[...]], axis=1)
    return _dot_nt(kcat, qcat)


def _c_key_block_exact(qn_ref, qr_ref, kn_ref, kr_ref, vt_ref, m_ref, l_ref, acc_ref):
    for hd in range(C_HEADS):
        hs = slice(hd * LANE, (hd + 1) * LANE)
        st = _c_scores(hd, qn_ref, qr_ref, kn_ref, kr_ref)
        m_old = m_ref[hd:hd + 1, :]
        m_new = jnp.maximum(m_old, jnp.max(st, axis=0, keepdims=True))
        alpha = jnp.exp2(m_old - m_new)
        p = jnp.exp2(st - m_new)
        l_ref[hd:hd + 1, :] = alpha * l_ref[hd:hd + 1, :] + jnp.sum(p, axis=0, keepdims=True)
        acc_ref[hd] = alpha * acc_ref[hd] + _dot(vt_ref[hs, :], p.astype(BF16))
        m_ref[hd:hd + 1, :] = m_new


def _c_key_block_fixed_ref(qn_ref, qr_ref, kn_ref, kr_ref, vt_ref, m_ref, ps_ref, pv_ref):
    for hd in range(C_HEADS):
        hs = slice(hd * LANE, (hd + 1) * LANE)
        p = jnp.exp2(_c_scores(hd, qn_ref, qr_ref, kn_ref, kr_ref) - m_ref[hd:hd + 1, :])
        ps_ref[hd:hd + 1, :] = jnp.sum(p, axis=0, keepdims=True)
        pv_ref[hd] = _dot(vt_ref[hs, :], p.astype(BF16))


def _c_init(m_ref, l_ref, acc_ref):
    m_ref[...] = jnp.full(m_ref.shape, NEG, F32)
    l_ref[...] = jnp.zeros(l_ref.shape, F32)
    acc_ref[...] = jnp.zeros(acc_ref.shape, F32)


def _c_finish(l_ref, acc_ref, o_ref):
    for hd in range(C_HEADS):
        ot = acc_ref[hd] / l_ref[hd:hd + 1, :]
        o_ref[:, hd * C_VDIM:(hd + 1) * C_VDIM] = ot.T.astype(o_ref.dtype)


def _c_attn_lat_kernel(qn_ref, qr_ref, kn_ref, kr_ref, vt_ref, knx_ref, krx_ref, vtx_ref, o_ref,
                       m_ref, l_ref, acc_ref, ps_ref, pv_ref):
    j = pl.program_id(1)

    @pl.when(j == 0)
    def _():
        _c_init(m_ref, l_ref, acc_ref)
        _c_key_block_exact(qn_ref, qr_ref, knx_ref, krx_ref, vtx_ref, m_ref, l_ref, acc_ref)

    _c_key_block_fixed_ref(qn_ref, qr_ref, kn_ref, kr_ref, vt_ref, m_ref, ps_ref, pv_ref)
    safe = jnp.max(ps_ref[...]) < C_SAFE_SUM

    @pl.when(safe)
    def _():
        l_ref[...] = l_ref[...] + ps_ref[...]
        acc_ref[...] = acc_ref[...] + pv_ref[...]

    @pl.when(jnp.logical_not(safe))
    def _():
        _c_key_block_exact(qn_ref, qr_ref, kn_ref, kr_ref, vt_ref, m_ref, l_ref, acc_ref)

    @pl.when(j == pl.num_programs(1) - 1)
    def _():
        _c_finish(l_ref, acc_ref, o_ref)


def _c_attn_ctx_kernel(qn_ref, qr_ref, knx_ref, krx_ref, vtx_ref, o_ref, m_ref, l_ref, acc_ref):
    _c_init(m_ref, l_ref, acc_ref)
    _c_key_block_exact(qn_ref, qr_ref, knx_ref, krx_ref, vtx_ref, m_ref, l_ref, acc_ref)
    _c_finish(l_ref, acc_ref, o_ref)


def _c_scratch(tq, with_uncommitted):
    stats = [pltpu.VMEM((C_HEADS, tq), F32), pltpu.VMEM((C_HEADS, tq), F32), pltpu.VMEM((C_HEADS, C_VDIM, tq), F32)]
    return stats + (stats[1:] if with_uncommitted else [])


def _c_attn(qn, qr, kn, kr, vt, n_lat, n_ctx, with_ctx_queries):
    tq = min(512, n_lat)
    tk = min(2048, n_lat)
    hw = C_HEADS * C_NOPE
    cb = n_lat // n_ctx
    qrow = lambda i, j: (i, 0)
    krow = lambda i, j: (j, 0)
    o = pl.pallas_call(
        _c_attn_lat_kernel,
        grid=(n_lat // tq, n_lat // tk),
        in_specs=[
            pl.BlockSpec((tq, hw), qrow),
            pl.BlockSpec((tq, C_QR_PAD), qrow),
            pl.BlockSpec((tk, hw), krow),
            pl.BlockSpec((tk, LANE), krow),
            pl.BlockSpec((hw, tk), lambda i, j: (0, j)),
            pl.BlockSpec((n_ctx, hw), lambda i, j: (cb, 0)),
            pl.BlockSpec((n_ctx, LANE), lambda i, j: (cb, 0)),
            pl.BlockSpec((hw, n_ctx), lambda i, j: (0, cb)),
        ],
        out_specs=pl.BlockSpec((tq, hw), qrow),
        out_shape=jax.ShapeDtypeStruct((n_lat, hw), BF16),
        scratch_shapes=_c_scratch(tq, True),
        compiler_params=_params("parallel", "arbitrary"),
        name="c_attn_lat",
    )(qn, qr, kn, kr, vt, kn, kr, vt)
    if not with_ctx_queries:
        return o, o, 0
    o_ctx = pl.pallas_call(
        _c_attn_ctx_kernel,
        grid=(1,),
        in_specs=[
            pl.BlockSpec((n_ctx, hw), lambda i: (cb, 0)),
            pl.BlockSpec((n_ctx, C_QR_PAD), lambda i: (cb, 0)),
            pl.BlockSpec((n_ctx, hw), lambda i: (cb, 0)),
            pl.BlockSpec((n_ctx, LANE), lambda i: (cb, 0)),
            pl.BlockSpec((hw, n_ctx), lambda i: (0, cb)),
        ],
        out_specs=pl.BlockSpec((n_ctx, hw), lambda i: (0, 0)),
        out_shape=jax.ShapeDtypeStruct((n_ctx, hw), BF16),
        scratch_shapes=_c_scratch(n_ctx, False),
        compiler_params=_params("arbitrary"),
        name="c_attn_ctx",
    )(qn, qr, kn, kr, vt)
    return o, o_ctx, 0


def _rope_tables(n_lat, d_rot):
    n_rows = n_lat // GRID_W
    d_axis = d_rot // 2
    quarter = d_rot // 4
    inv = 1.0 / (ROPE_BASE ** (jnp.arange(0, d_axis, 2, dtype=F32) / d_axis))
    ar = jnp.arange(n_rows, dtype=F32)[:, None] * inv
    ac = jnp.arange(GRID_W, dtype=F32)[:, None] * inv

    def rows(a, b):
        t = jnp.pad(jnp.concatenate([a, b], axis=1), ((0, 0), (0, LANE - 2 * quarter)))
        t = t.reshape(n_rows // ROWS_PER_TILE, ROWS_PER_TILE, LANE)
        return jnp.pad(t, ((0, 1), (0, SUBLANE - ROWS_PER_TILE), (0, 0)))

    def cols(a, b):
        t = jnp.pad(jnp.concatenate([a, b], axis=1), ((0, 0), (2 * quarter, LANE - 4 * quarter)))
        return jnp.tile(t, (ROWS_PER_TILE, 1))

    cr, sr, cc, sc = jnp.cos(ar), jnp.sin(ar), jnp.cos(ac), jnp.sin(ac)
    return rows(cr, cr), rows(-sr, sr), cols(cc, cc), cols(-sc, sc)


def _rope_specs():
    row = pl.BlockSpec((1, SUBLANE, LANE), lambda i: (i, 0, 0))
    return [row, row, _resident((ROW_TILE, LANE)), _resident((ROW_TILE, LANE))]


def _rope_tile(is_ctx, row_cos_ref, row_sin_ref, col_cos_ref, col_sin_ref):
    def spread(ref):
        t = ref[0]
        return jnp.concatenate([jnp.broadcast_to(t[r:r + 1, :], (GRID_W, LANE)) for r in range(ROWS_PER_TILE)], axis=0)
    cos = jnp.where(is_ctx, 1.0, spread(row_cos_ref) + col_cos_ref[...])
    sin = jnp.where(is_ctx, 0.0, spread(row_sin_ref) + col_sin_ref[...])
    return cos, sin


def _pad_cols(w, width):
    return jnp.pad(w, ((0, 0), (0, width - w.shape[1])))


def kernel(x, c, ctx, c_ctx, ada_w, ada_b, norm_g, ffn_w_gu, ffn_w_down, a_w_qkv, a_sinks, a_w_o, b_w_in, b_conv_w, b_conv_b, b_lam, b_w_a, b_b_a, b_w_x, b_b_x, b_w_out, c_w_in, c_g_q, c_g_kv, c_w_uq, c_w_ukv, c_w_out):
    assert x.shape[0] == 1 and ctx.shape[0] == 1 and x.shape[2] == D_MODEL
    n_lat, n_ctx = x.shape[1], ctx.shape[1]
    assert n_ctx == ROW_TILE and n_lat % ROW_TILE == 0 and n_lat % GRID_W == 0
    n_lat_tiles = n_lat // ROW_TILE
    n_all_tiles = n_lat_tiles + 1

    x_stream = (x[0], ctx[0], 0)
    s_rows = jnp.zeros((SUBLANE, D_MODEL), F32).at[0].set(c[0]).at[1].set(c_ctx)
    mods_all = _ada(s_rows, ada_w, ada_b)[:, :2].reshape(DEPTH, 2, N_MOD, D_MODEL)
    w_gu_all = ffn_w_gu.astype(BF16)
    w_down_all = ffn_w_down.astype(BF16)
    rope_a = _rope_tables(n_lat, A_HEAD_DIM)
    rope_c = _rope_tables(n_lat, C_ROPE)

    for i in range(DEPTH):
        last = i == DEPTH - 1
        kind, j = i % N_MIXERS, i // N_MIXERS
        mods = mods_all[i]
        g = norm_g[i]
        if kind == 0:
            q, k, v = _a_proj(x_stream, mods, g, a_w_qkv[j].astype(BF16), rope_a, n_lat_tiles)
            o = _a_attn(q, k, v, a_sinks[j], n_lat, n_ctx, not last)
            o_stream = _stream(o, n_lat_tiles) if not last else (o, o, 0)
            w_o = a_w_o[j]
        elif kind == 1:
            xa = x_stream[0]
            xpre, gate = _b_proj(xa, mods, g, b_w_in[j].astype(BF16), n_lat_tiles)
            scan_args = lambda d: (b_conv_w[j], b_conv_b[j][None], b_lam[j, d][None], b_w_a[j, d].astype(BF16),
                                   b_b_a[j, d][None], b_w_x[j, d].astype(BF16), b_b_x[j, d][None], n_lat_tiles)
            hf = _b_scan(False, xpre, *scan_args(0))
            o_stream = _stream(_b_scan(True, xpre, *scan_args(1), hf=hf, gate=gate), n_lat_tiles)
            w_o = b_w_out[j]
        else:
            xa = x_stream[0]
            w_in = c_w_in[j]
            w_in_p = _pad_cols(w_in, C_IN_PAD).astype(BF16)
            w_uq3 = c_w_uq[j].reshape(C_Q_RANK, C_HEADS, C_NOPE + C_ROPE)
            w_uq_p = jnp.concatenate(
                [w_uq3[..., :C_NOPE].reshape(C_Q_RANK, C_HEADS * C_NOPE),
                 jnp.pad(w_uq3[..., C_NOPE:], ((0, 0), (0, 0), (0, LANE - C_ROPE))).reshape(C_Q_RANK, C_QR_PAD)],
                axis=1).astype(BF16)
            w_ukv3 = c_w_ukv[j].reshape(C_KV_RANK, C_HEADS, C_NOPE + C_VDIM)
            w_uk = w_ukv3[..., :C_NOPE].reshape(C_KV_RANK, C_HEADS * C_NOPE).astype(BF16)
            w_uvt = w_ukv3[..., C_NOPE:].reshape(C_KV_RANK, C_HEADS * C_VDIM).T.astype(BF16)
            qn, qr, kn, kr, vt = _c_proj(xa, mods, g, w_in_p, c_g_q[j][None], c_g_kv[j][None], w_uq_p, w_uk, w_uvt,
                                         rope_c, n_lat_tiles)
            o_stream = _c_attn(qn, qr, kn, kr, vt, n_lat, n_ctx, not last)
            w_o = c_w_out[j]
        xa = _post(o_stream, x_stream, mods, g, w_o.astype(BF16), w_gu_all, w_down_all, i, n_lat_tiles,
                   n_lat_tiles if last else n_all_tiles)
        x_stream = _stream(xa, n_lat_tiles)
    return xa[None, :n_lat]
```
